```python
import math
import jax, jax.numpy as jnp
from jax import lax
import numpy as np

D_MODEL = 1024
BATCH = 4
SEQ = 8192
DEPTH = 2
DEC_BATCH = 8
DEC_SEQ = 16
PAST_LEN = 1024

CHUNK = 64
EPS = 1e-6
A_WIDTH = 256
A_GROUPS = 4
A_GDIM = A_WIDTH // A_GROUPS
A_CHUNK = 128
B_WIDTH = 256
B_KW = 31
C_HEADS = 8
C_NOPE = 64
C_ROPE = 32
C_VDIM = 64
C_QRANK = 256
C_KVRANK = 128
C_WIDTH = C_HEADS * C_VDIM
ROPE_THETA = 10000.0
Q_BLOCK = 128
MEM_LEN = 256
MEM_HEADS = 4
MEM_HDIM = 128
MEM_WIDTH = MEM_HEADS * MEM_HDIM
D_FF = 4 * D_MODEL

IN_WIDTH = 2 * A_WIDTH + 2 * B_WIDTH + C_QRANK + C_KVRANK + C_ROPE
MIX_WIDTH = A_WIDTH + B_WIDTH + C_WIDTH

kernel_name = 'hybrid_streaming_encoder_step'


def _rmsnorm(x, g):
    xf = x.astype(jnp.float32)
    y = xf * lax.rsqrt(jnp.mean(xf * xf, axis=-1, keepdims=True) + EPS)
    return (y * g.astype(jnp.float32)).astype(x.dtype)


def _layernorm(x, g, b):
    xf = x.astype(jnp.float32)
    xc = xf - jnp.mean(xf, axis=-1, keepdims=True)
    y = xc * lax.rsqrt(jnp.mean(xc * xc, axis=-1, keepdims=True) + EPS)
    return (y * g.astype(jnp.float32) + b.astype(jnp.float32)).astype(x.dtype)


def _rope(x, pos):
    half = C_ROPE // 2
    inv = ROPE_THETA ** (-jnp.arange(half, dtype=jnp.float32) / half)
    ang = pos.astype(jnp.float32)[:, None] * inv[None, :]
    shape = (pos.shape[0],) + (1,) * (x.ndim - 3) + (half,)
    cos = jnp.cos(ang).reshape(shape)
    sin = jnp.sin(ang).reshape(shape)
    xf = x.astype(jnp.float32)
    x1, x2 = xf[..., :half], xf[..., half:]
    return jnp.concatenate([x1 * cos - x2 * sin, x1 * sin + x2 * cos], axis=-1).astype(x.dtype)


def _gmlp(u, v, g, ws, bs):
    b, t, _ = u.shape
    L = min(t, A_CHUNK)
    u = jax.nn.gelu(u, approximate=False)
    v = _rmsnorm(jax.nn.gelu(v, approximate=False), g)
    mask = jnp.tril(jnp.ones((A_CHUNK, A_CHUNK), dtype=bool))
    wm = jnp.where(mask[None], ws, jnp.zeros_like(ws))[:, :L, :L]
    vc = v.reshape(b, t // L, L, A_GROUPS, A_GDIM)
    mixed = jnp.einsum('gij,bcjgd->bcigd', wm, vc) + bs[:, :L].T[None, None, :, :, None]
    return u * mixed.reshape(b, t, A_WIDTH), v


def _conv_module(a, gate, hist, w, bias, ln_g, ln_b):
    z = a * jax.nn.sigmoid(gate)
    if hist is None:
        hist = jnp.zeros((z.shape[0], B_KW - 1, B_WIDTH), z.dtype)
    zp = jnp.concatenate([hist.astype(z.dtype), z], axis=1)
    y = lax.conv_general_dilated(zp, w[:, None, :].astype(z.dtype), window_strides=(1,), padding='VALID',
                                 dimension_numbers=('NWC', 'WIO', 'NWC'),
                                 feature_group_count=B_WIDTH) + bias
    y = jax.nn.silu(_layernorm(y, ln_g, ln_b))
    return y, zp[:, zp.shape[1] - (B_KW - 1):]


def _mla_attend(q_nope, q_rope, qpos, k_nope, k_rope, v, kpos):
    s = jnp.einsum('bqhd,bkhd->bhqk', q_nope, k_nope) + jnp.einsum('bqhr,bkr->bhqk', q_rope, k_rope)
    s = s.astype(jnp.float32) * (1.0 / math.sqrt(C_NOPE + C_ROPE))
    mask = (kpos[None, :] // CHUNK) <= (qpos[:, None] // CHUNK)
    s = jnp.where(mask[None, None], s, -1e30)
    p = jax.nn.softmax(s, axis=-1).astype(v.dtype)
    o = jnp.einsum('bhqk,bkhd->bqhd', p, v)
    return o.reshape(o.shape[0], o.shape[1], C_WIDTH)


def _mem_kv(mem, l, W):
    b, m, _ = mem.shape
    mn = _rmsnorm(mem, W['mem_norm_g'][l])
    k = _rmsnorm((mn @ W['w_mk'][l]).reshape(b, m, MEM_HEADS, MEM_HDIM), W['m_k_g'][l])
    v = (mn @ W['w_mv'][l]).reshape(b, m, MEM_HEADS, MEM_HDIM)
    return k, v


def _layer(x, l, W, mem_k, mem_v, conv_hist, lat_past, kr_past, start):
    b, t, _ = x.shape
    h = _rmsnorm(x, W['norm_mix_g'][l]) @ W['w_in'][l]
    c0 = 2 * A_WIDTH + 2 * B_WIDTH
    cuts = [A_WIDTH, 2 * A_WIDTH, 2 * A_WIDTH + B_WIDTH, c0, c0 + C_QRANK, c0 + C_QRANK + C_KVRANK]
    a_u, a_v, b_a, b_g, c_q, c_kv, c_kr = jnp.split(h, cuts, axis=-1)
    y_a, v_rows = _gmlp(a_u, a_v, W['a_norm_g'][l], W['a_ws'][l], W['a_bs'][l])
    y_b, conv_state = _conv_module(b_a, b_g, conv_hist, W['b_dw_w'][l], W['b_dw_b'][l],
                                   W['b_ln_g'][l], W['b_ln_b'][l])
    pos = start + jnp.arange(t)
    q = (_rmsnorm(c_q, W['c_qa_g'][l]) @ W['c_w_uq'][l]).reshape(b, t, C_HEADS, C_NOPE + C_ROPE)
    q_nope = _rmsnorm(q[..., :C_NOPE], W['c_qn_g'][l])
    q_rope = _rope(_rmsnorm(q[..., C_NOPE:], W['c_qr_g'][l]), pos)
    lat_new = _rmsnorm(c_kv, W['c_kva_g'][l])
    kr_new = _rope(_rmsnorm(c_kr, W['c_kr_g'][l]), pos)
    if lat_past is None:
        lat_all, kr_all, kpos = lat_new, kr_new, pos
    else:
        lat_all = jnp.concatenate([lat_past.astype(lat_new.dtype), lat_new], axis=1)
        kr_all = jnp.concatenate([kr_past.astype(kr_new.dtype), kr_new], axis=1)
        kpos = jnp.arange(lat_past.shape[1] + t)
    kv = (lat_all @ W['c_w_ukv'][l]).reshape(b, lat_all.shape[1], C_HEADS, C_NOPE + C_VDIM)
    k_nope = _rmsnorm(kv[..., :C_NOPE], W['c_kn_g'][l])
    v = kv[..., C_NOPE:]
    if t > Q_BLOCK and t % Q_BLOCK == 0:
        nb = t // Q_BLOCK
        qn_b = q_nope.reshape(b, nb, Q_BLOCK, C_HEADS, C_NOPE).transpose(1, 0, 2, 3, 4)
        qr_b = q_rope.reshape(b, nb, Q_BLOCK, C_HEADS, C_ROPE).transpose(1, 0, 2, 3, 4)
        qp_b = pos.reshape(nb, Q_BLOCK)
        y_c = lax.map(lambda xs: _mla_attend(xs[0], xs[1], xs[2], k_nope, kr_all, v, kpos), (qn_b, qr_b, qp_b))
        y_c = y_c.transpose(1, 0, 2, 3).reshape(b, t, C_WIDTH)
    else:
        y_c = _mla_attend(q_nope, q_rope, pos, k_nope, kr_all, v, kpos)
    x = x + jnp.concatenate([y_a, y_b, y_c], axis=-1) @ W['w_out'][l]
    qm = _rmsnorm((_rmsnorm(x, W['norm_mem_g'][l]) @ W['w_mq'][l]).reshape(b, t, MEM_HEADS, MEM_HDIM),
                  W['m_q_g'][l])
    sm = jnp.einsum('bqhd,bkhd->bhqk', qm, mem_k.astype(qm.dtype)).astype(jnp.float32) * (1.0 / math.sqrt(MEM_HDIM))
    pm = jax.nn.softmax(sm, axis=-1).astype(x.dtype)
    om = jnp.einsum('bhqk,bkhd->bqhd', pm, mem_v.astype(x.dtype)).reshape(b, t, MEM_WIDTH)
    x = x + om @ W['w_mo'][l]
    x = x + jnp.square(jax.nn.relu(_rmsnorm(x, W['norm_ffn_g'][l]) @ W['w_ff1'][l])) @ W['w_ff2'][l]
    return x, v_rows, conv_state, lat_new, kr_new


def setup_inputs(seed: int = 0) -> dict:
    key = jax.random.key(seed)
    ks = iter(jax.random.split(key, 64))

    def nrm(shape, scale=1.0):
        return scale * jax.random.normal(next(ks), shape, jnp.float32)

    def gain(shape):
        return 1.0 + nrm(shape, 0.02)

    return {
        'x_prompt': nrm((BATCH, SEQ, D_MODEL)),
        'x_sample': nrm((DEC_BATCH, DEC_SEQ, D_MODEL)),
        'mem_prompt': nrm((BATCH, MEM_LEN, D_MODEL)),
        'cache_mla_latent': nrm((DEPTH, DEC_BATCH, PAST_LEN, C_KVRANK)),
        'cache_mla_krope': nrm((DEPTH, DEC_BATCH, PAST_LEN, C_ROPE)),
        'cache_conv': nrm((DEPTH, DEC_BATCH, B_KW - 1, B_WIDTH), 0.5),
        'cache_mem_k': nrm((DEPTH, DEC_BATCH, MEM_LEN, MEM_HEADS, MEM_HDIM)),
        'cache_mem_v': nrm((DEPTH, DEC_BATCH, MEM_LEN, MEM_HEADS, MEM_HDIM)),
        'norm_mix_g': gain((DEPTH, D_MODEL)),
        'w_in': nrm((DEPTH, D_MODEL, IN_WIDTH), D_MODEL ** -0.5),
        'a_norm_g': gain((DEPTH, A_WIDTH)),
        'a_ws': nrm((DEPTH, A_GROUPS, A_CHUNK, A_CHUNK), A_CHUNK ** -0.5),
        'a_bs': gain((DEPTH, A_GROUPS, A_CHUNK)),
        'b_dw_w': nrm((DEPTH, B_KW, B_WIDTH), B_KW ** -0.5),
        'b_dw_b': nrm((DEPTH, B_WIDTH), 0.02),
        'b_ln_g': gain((DEPTH, B_WIDTH)),
        'b_ln_b': nrm((DEPTH, B_WIDTH), 0.02),
        'c_qa_g': gain((DEPTH, C_QRANK)),
        'c_w_uq': nrm((DEPTH, C_QRANK, C_HEADS * (C_NOPE + C_ROPE)), C_QRANK ** -0.5),
        'c_kva_g': gain((DEPTH, C_KVRANK)),
        'c_w_ukv': nrm((DEPTH, C_KVRANK, C_HEADS * (C_NOPE + C_VDIM)), C_KVRANK ** -0.5),
        'c_qn_g': gain((DEPTH, C_NOPE)),
        'c_qr_g': gain((DEPTH, C_ROPE)),
        'c_kn_g': gain((DEPTH, C_NOPE)),
        'c_kr_g': gain((DEPTH, C_ROPE)),
        'w_out': nrm((DEPTH, MIX_WIDTH, D_MODEL), MIX_WIDTH ** -0.5),
        'norm_mem_g': gain((DEPTH, D_MODEL)),
        'mem_norm_g': gain((DEPTH, D_MODEL)),
        'w_mq': nrm((DEPTH, D_MODEL, MEM_WIDTH), D_MODEL ** -0.5),
        'w_mk': nrm((DEPTH, D_MODEL, MEM_WIDTH), D_MODEL ** -0.5),
        'w_mv': nrm((DEPTH, D_MODEL, MEM_WIDTH), D_MODEL ** -0.5),
        'w_mo': nrm((DEPTH, MEM_WIDTH, D_MODEL), MEM_WIDTH ** -0.5),
        'm_q_g': gain((DEPTH, MEM_HDIM)),
        'm_k_g': gain((DEPTH, MEM_HDIM)),
        'norm_ffn_g': gain((DEPTH, D_MODEL)),
        'w_ff1': nrm((DEPTH, D_MODEL, D_FF), D_MODEL ** -0.5),
        'w_ff2': nrm((DEPTH, D_FF, D_MODEL), D_FF ** -0.5),
    }


def reference(x_prompt, x_sample, mem_prompt, cache_mla_latent, cache_mla_krope, cache_conv,
              cache_mem_k, cache_mem_v, norm_mix_g, w_in, a_norm_g, a_ws, a_bs, b_dw_w, b_dw_b,
              b_ln_g, b_ln_b, c_qa_g, c_w_uq, c_kva_g, c_w_ukv, c_qn_g, c_qr_g, c_kn_g, c_kr_g,
              w_out, norm_mem_g, mem_norm_g, w_mq, w_mk, w_mv, w_mo, m_q_g, m_k_g, norm_ffn_g,
              w_ff1, w_ff2):
    W = dict(norm_mix_g=norm_mix_g, w_in=w_in, a_norm_g=a_norm_g, a_ws=a_ws, a_bs=a_bs,
             b_dw_w=b_dw_w, b_dw_b=b_dw_b, b_ln_g=b_ln_g, b_ln_b=b_ln_b, c_qa_g=c_qa_g,
             c_w_uq=c_w_uq, c_kva_g=c_kva_g, c_w_ukv=c_w_ukv, c_qn_g=c_qn_g, c_qr_g=c_qr_g,
             c_kn_g=c_kn_g, c_kr_g=c_kr_g, w_out=w_out, norm_mem_g=norm_mem_g,
             mem_norm_g=mem_norm_g, w_mq=w_mq, w_mk=w_mk, w_mv=w_mv, w_mo=w_mo, m_q_g=m_q_g,
             m_k_g=m_k_g, norm_ffn_g=norm_ffn_g, w_ff1=w_ff1, w_ff2=w_ff2)
    past = cache_mla_latent.shape[2]
    hp, hs = x_prompt, x_sample
    lat_p, kr_p, conv_p, mk_p, mv_p = [], [], [], [], []
    lat_s, kr_s, conv_s, gv_s = [], [], [], []
    for l in range(DEPTH):
        mk, mv = _mem_kv(mem_prompt, l, W)
        hp, _, cst, lat, kr = _layer(hp, l, W, mk, mv, None, None, None, 0)
        lat_p.append(lat); kr_p.append(kr); conv_p.append(cst); mk_p.append(mk); mv_p.append(mv)
        hs, gv, cst, lat, kr = _layer(hs, l, W, cache_mem_k[l], cache_mem_v[l], cache_conv[l],
                                      cache_mla_latent[l], cache_mla_krope[l], past)
        lat_s.append(lat); kr_s.append(kr); conv_s.append(cst); gv_s.append(gv)
    return (hp, hs,
            jnp.stack(lat_p), jnp.stack(kr_p), jnp.stack(conv_p), jnp.stack(mk_p), jnp.stack(mv_p),
            jnp.stack(lat_s), jnp.stack(kr_s), jnp.stack(conv_s), jnp.stack(gv_s))
```

```python
import functools
import math

import jax
import jax.numpy as jnp
from jax import lax
from jax.experimental import pallas as pl
from jax.experimental.pallas import tpu as pltpu

F32 = jnp.float32
BF16 = jnp.bfloat16

EPS = 1e-6
CHUNK_SHIFT = 6
A_WIDTH = 256
A_GROUPS = 4
A_GDIM = 64
A_CHUNK = 128
B_WIDTH = 256
B_KW = 31
C_HEADS = 8
C_NOPE = 64
C_ROPE = 32
C_VDIM = 64
C_QRANK = 256
C_KVRANK = 128
ROPE_THETA = 10000.0
MEM_HEADS = 4
MEM_HDIM = 128
HEAD_PAD = 128
IN_PAD = 1536
HIST_PAD = 32
QK_SCALE = math.log2(math.e) / math.sqrt(C_NOPE + C_ROPE)

VMEM_LIMIT = 56 * 1024 * 1024


def _cparams(n_axes):
    return pltpu.CompilerParams(dimension_semantics=("arbitrary",) * n_axes,
                                vmem_limit_bytes=VMEM_LIMIT)


def _rms_lane(x, g):
    return x * lax.rsqrt(jnp.mean(x * x, axis=-1, keepdims=True) + EPS) * g


def _gelu(x):
    return 0.5 * x * (1.0 + lax.erf(x * (1.0 / math.sqrt(2.0))))


def _const_spec(shape):
    nd = len(shape)
    return pl.BlockSpec(shape, lambda *_: (0,) * nd, pipeline_mode=pl.Buffered(1))


def _inproj_kernel(x_ref, hist_ref, cosq_ref, sinq_ref, cosk_ref, sink_ref,
                   gmix_ref, win_ref, ag_ref, aws_ref, abias_ref,
                   dww_ref, dwb_ref, lng_ref, lnb_ref,
                   qag_ref, wuqT_ref, qng_ref, qrg_ref, kvag_ref, krg_ref,
                   ya_ref, yb_ref, qT_ref, lat_ref, kr_ref, cs_ref, gv_ref,
                   zbuf, *, tm, L):
    t = pl.program_id(1)
    x = x_ref[0]
    xn = _rms_lane(x, gmix_ref[...])
    h = jnp.dot(xn.astype(BF16), win_ref[...], preferred_element_type=F32)
    a_u = h[:, 0:256]
    a_v = h[:, 256:512]
    b_a = h[:, 512:768]
    b_g = h[:, 768:1024]
    c_q = h[:, 1024:1280]
    c_kv = h[:, 1280:1408]
    c_kr = h[:, 1408:1536]

    u = _gelu(a_u)
    v = _rms_lane(_gelu(a_v), ag_ref[...])
    gv_ref[0] = v
    vb = v.astype(BF16)
    row = lax.broadcasted_iota(jnp.int32, (L, L), 0)
    col = lax.broadcasted_iota(jnp.int32, (L, L), 1)
    tril = col <= row
    wcat = jnp.concatenate(
        [jnp.where(tril, aws_ref[g], 0.0).astype(BF16) for g in range(A_GROUPS)], axis=1)
    lane = lax.broadcasted_iota(jnp.int32, (L, A_WIDTH), 1)
    gmasks = [(lane >= g * A_GDIM) & (lane < (g + 1) * A_GDIM) for g in range(A_GROUPS)]
    zero_b = jnp.zeros((L, A_WIDTH), BF16)
    bias = abias_ref[...]
    for c in range(tm // L):
        vc = vb[c * L:(c + 1) * L]
        vstack = jnp.concatenate([jnp.where(gmasks[g], vc, zero_b) for g in range(A_GROUPS)], axis=0)
        mixed = jnp.dot(wcat, vstack, preferred_element_type=F32) + bias
        ya_ref[0, c * L:(c + 1) * L, :] = (u[c * L:(c + 1) * L] * mixed).astype(ya_ref.dtype)

    z = b_a * jax.nn.sigmoid(b_g)

    @pl.when(t == 0)
    def _():
        zbuf[0:8, :] = jnp.zeros((8, B_WIDTH), F32)
        zbuf[HIST_PAD - (B_KW - 1):HIST_PAD, :] = hist_ref[0]

    @pl.when(t > 0)
    def _():
        zbuf[0:HIST_PAD, :] = zbuf[tm:tm + HIST_PAD, :]

    zbuf[HIST_PAD:HIST_PAD + tm, :] = z
    off = HIST_PAD - (B_KW - 1)
    acc = jnp.zeros((tm, B_WIDTH), F32) + dwb_ref[...]
    for j in range(B_KW):
        acc = acc + dww_ref[j:j + 1, :] * zbuf[off + j:off + j + tm, :]
    cs_ref[0] = zbuf[tm + off:tm + HIST_PAD, :]
    mu = jnp.mean(acc, axis=-1, keepdims=True)
    xc = acc - mu
    yn = xc * lax.rsqrt(jnp.mean(xc * xc, axis=-1, keepdims=True) + EPS) * lng_ref[...] + lnb_ref[...]
    yb_ref[0] = (yn * jax.nn.sigmoid(yn)).astype(yb_ref.dtype)

    qa = _rms_lane(c_q, qag_ref[...]).astype(BF16)
    qT = lax.dot_general(wuqT_ref[...], qa, (((1,), (1,)), ((), ())),
                         preferred_element_type=F32)
    cq = cosq_ref[...]
    sq = sinq_ref[...]
    gqn = qng_ref[...] * QK_SCALE
    gqr = qrg_ref[...]
    half = C_ROPE // 2
    zpad = jnp.zeros((HEAD_PAD - C_NOPE - C_ROPE, tm), F32)
    for hh in range(C_HEADS):
        blk = qT[hh * HEAD_PAD:(hh + 1) * HEAD_PAD]
        qn = blk[0:C_NOPE]
        qr = blk[C_NOPE:C_NOPE + C_ROPE]
        qn = qn * lax.rsqrt(jnp.mean(qn * qn, axis=0, keepdims=True) + EPS) * gqn
        qr = qr * lax.rsqrt(jnp.mean(qr * qr, axis=0, keepdims=True) + EPS) * gqr
        x1 = qr[0:half]
        x2 = qr[half:C_ROPE]
        r1 = (x1 * cq - x2 * sq) * QK_SCALE
        r2 = (x1 * sq + x2 * cq) * QK_SCALE
        qT_ref[0, hh] = jnp.concatenate([qn, r1, r2, zpad], axis=0).astype(qT_ref.dtype)

    lat_ref[0] = _rms_lane(c_kv, kvag_ref[...])
    krn = c_kr * lax.rsqrt(jnp.sum(c_kr * c_kr, axis=-1, keepdims=True) * (1.0 / C_ROPE) + EPS) * krg_ref[...]
    lane_k = lax.broadcasted_iota(jnp.int32, krn.shape, 1)
    partner = jnp.where(lane_k < half, pltpu.roll(krn, 128 - half, 1), pltpu.roll(krn, half, 1))
    kro = krn * cosk_ref[...] + partner * sink_ref[...]
    kr_ref[0] = kro[:, 0:C_ROPE]


def _inproj(x, hist, tabs, P, *, tm, L):
    B, T, D = x.shape
    nt = T // tm
    cosq, sinq, cosk, sink = tabs
    kern = functools.partial(_inproj_kernel, tm=tm, L=L)
    tile = lambda w: pl.BlockSpec((1, tm, w), lambda b, t: (b, t, 0))
    in_specs = [
        tile(D),
        pl.BlockSpec((1, B_KW - 1, B_WIDTH), lambda b, t: (b, 0, 0)),
        pl.BlockSpec((C_ROPE // 2, tm), lambda b, t: (0, t)),
        pl.BlockSpec((C_ROPE // 2, tm), lambda b, t: (0, t)),
        pl.BlockSpec((tm, 128), lambda b, t: (t, 0)),
        pl.BlockSpec((tm, 128), lambda b, t: (t, 0)),
    ]
    consts = [P['g_mix'], P['w_in'], P['a_g'], P['a_ws'], P['a_bias'],
              P['dw_w'], P['dw_b'], P['ln_g'], P['ln_b'],
              P['qa_g'], P['w_uqT'], P['qn_g'], P['qr_g'], P['kva_g'], P['kr_g']]
    in_specs += [_const_spec(c.shape) for c in consts]
    out_shape = (
        jax.ShapeDtypeStruct((B, T, A_WIDTH), BF16),
        jax.ShapeDtypeStruct((B, T, B_WIDTH), BF16),
        jax.ShapeDtypeStruct((B, C_HEADS, HEAD_PAD, T), BF16),
        jax.ShapeDtypeStruct((B, T, C_KVRANK), F32),
        jax.ShapeDtypeStruct((B, T, C_ROPE), F32),
        jax.ShapeDtypeStruct((B, B_KW - 1, B_WIDTH), F32),
        jax.ShapeDtypeStruct((B, T, A_WIDTH), F32),
    )
    out_specs = (
        tile(A_WIDTH), tile(B_WIDTH),
        pl.BlockSpec((1, C_HEADS, HEAD_PAD, tm), lambda b, t: (b, 0, 0, t)),
        tile(C_KVRANK), tile(C_ROPE),
        pl.BlockSpec((1, B_KW - 1, B_WIDTH), lambda b, t: (b, 0, 0)),
        tile(A_WIDTH),
    )
    return pl.pallas_call(
        kern, grid=(B, nt), in_specs=in_specs, out_specs=out_specs, out_shape=out_shape,
        scratch_shapes=[pltpu.VMEM((tm + HIST_PAD, B_WIDTH), F32)],
        compiler_params=_cparams(2), name=f"inproj_{tm}",
    )(x, hist, cosq, sinq, cosk, sink, *consts)


def _kvgen_kernel(lat_ref, krp_ref, wuk_ref, kng_ref, wuvT_ref, k_ref, vT_ref):
    latb = lat_ref[0].astype(BF16)
    kf = jnp.dot(latb, wuk_ref[...], preferred_element_type=F32)
    krp = krp_ref[0]
    g = kng_ref[...]
    vT = lax.dot_general(wuvT_ref[...], latb, (((1,), (1,)), ((), ())),
                         preferred_element_type=F32)
    for hh in range(C_HEADS):
        kh = kf[:, hh * HEAD_PAD:(hh + 1) * HEAD_PAD]
        ms = jnp.sum(kh * kh, axis=-1, keepdims=True) * (1.0 / C_NOPE)
        k_ref[0, hh, 0] = (kh * lax.rsqrt(ms + EPS) * g + krp).astype(k_ref.dtype)
        vT_ref[0, hh, 0] = vT[hh * C_VDIM:(hh + 1) * C_VDIM].astype(vT_ref.dtype)


def _kvgen(lat, krp, P, *, tk):
    B, Tk, _ = lat.shape
    nkt = Tk // tk
    consts = [P['w_uk'], P['kn_g'], P['w_uvT']]
    return pl.pallas_call(
        _kvgen_kernel, grid=(B, nkt),
        in_specs=[pl.BlockSpec((1, tk, C_KVRANK), lambda b, t: (b, t, 0)),
                  pl.BlockSpec((1, tk, HEAD_PAD), lambda b, t: (b, t, 0))]
                 + [_const_spec(c.shape) for c in consts],
        out_specs=(pl.BlockSpec((1, C_HEADS, 1, tk, HEAD_PAD), lambda b, t: (b, 0, t, 0, 0)),
                   pl.BlockSpec((1, C_HEADS, 1, C_VDIM, tk), lambda b, t: (b, 0, t, 0, 0))),
        out_shape=(jax.ShapeDtypeStruct((B, C_HEADS, nkt, tk, HEAD_PAD), BF16),
                   jax.ShapeDtypeStruct((B, C_HEADS, nkt, C_VDIM, tk), BF16)),
        compiler_params=_cparams(2), name=f"kvgen_{tk}",
    )(lat, krp, *consts)


def _attn_kernel(qT_ref, k_ref, vT_ref, o_ref, *, tq, tk, q_pos0, causal_tiles):
    qi = pl.program_id(2)
    qT = qT_ref[0, 0]

    def step(kt, carry, masked):
        m, l, acc = carry
        s = jnp.dot(k_ref[0, 0, kt], qT, preferred_element_type=F32)
        if masked:
            kpos = kt * tk + lax.broadcasted_iota(jnp.int32, (tk, 1), 0)
            qpos = q_pos0 + qi * tq + lax.broadcasted_iota(jnp.int32, (1, tq), 1)
            vis = lax.shift_right_logical(kpos, CHUNK_SHIFT) <= lax.shift_right_logical(qpos, CHUNK_SHIFT)
            s = jnp.where(vis, s, -1e30)
        m_new = jnp.maximum(m, jnp.max(s, axis=0, keepdims=True))
        alpha = jnp.exp2(m - m_new)
        p = jnp.exp2(s - m_new)
        l = alpha * l + jnp.sum(p, axis=0, keepdims=True)
        pv = jnp.dot(vT_ref[0, 0, kt], p.astype(BF16), preferred_element_type=F32)
        return m_new, l, alpha * acc + pv

    init = (jnp.full((1, tq), -1e30, F32), jnp.zeros((1, tq), F32), jnp.zeros((C_VDIM, tq), F32))
    if causal_tiles:
        carry = lax.fori_loop(0, qi, lambda kt, c: step(kt, c, False), init)
        carry = step(qi, carry, True)
    else:
        carry = step(0, init, True)
    _, l, acc = carry
    o_ref[0] = (acc * (1.0 / l)).astype(o_ref.dtype)


def _attn(qT, k, vT, *, tq, q_pos0, causal_tiles):
    B, H, _, Tq = qT.shape
    _, _, nkt, tk, _ = k.shape
    nq = Tq // tq
    kern = functools.partial(_attn_kernel, tq=tq, tk=tk, q_pos0=q_pos0, causal_tiles=causal_tiles)
    return pl.pallas_call(
        kern, grid=(B, H, nq),
        in_specs=[pl.BlockSpec((1, 1, HEAD_PAD, tq), lambda b, h, q: (b, h, 0, q)),
                  pl.BlockSpec((1, 1, nkt, tk, HEAD_PAD), lambda b, h, q: (b, h, 0, 0, 0)),
                  pl.BlockSpec((1, 1, nkt, C_VDIM, tk), lambda b, h, q: (b, h, 0, 0, 0))],
        out_specs=pl.BlockSpec((1, C_VDIM, tq), lambda b, h, q: (b, h, q)),
        out_shape=jax.ShapeDtypeStruct((B, H * C_VDIM, Tq), BF16),
        compiler_params=_cparams(3), name=f"attn_{tq}",
    )(qT, k, vT)


def _post_kernel(x_ref, ya_ref, yb_ref, ycT_ref, mk_ref, mv_ref,
                 wout_ref, gmem_ref, wmq_ref, mqg_ref, wmo_ref, gffn_ref, w1_ref, w2_ref,
                 o_ref, *, ff_chunk):
    x = x_ref[0]
    ab = A_WIDTH + B_WIDTH
    mix_ab = jnp.concatenate([ya_ref[0], yb_ref[0]], axis=1)
    x1 = x + jnp.dot(mix_ab, wout_ref[0:ab, :], preferred_element_type=F32)
    x1 = x1 + lax.dot_general(ycT_ref[0], wout_ref[ab:, :], (((0,), (0,)), ((), ())),
                              preferred_element_type=F32)
    qm = jnp.dot(_rms_lane(x1, gmem_ref[...]).astype(BF16), wmq_ref[...], preferred_element_type=F32)
    mk = mk_ref[0].astype(BF16)
    mv = mv_ref[0].astype(BF16)
    outs = []
    for hh in range(MEM_HEADS):
        sl = slice(hh * MEM_HDIM, (hh + 1) * MEM_HDIM)
        qh = _rms_lane(qm[:, sl], mqg_ref[...]).astype(BF16)
        s = lax.dot_general(qh, mk[:, sl], (((1,), (1,)), ((), ())),
                            preferred_element_type=F32) * (1.0 / math.sqrt(MEM_HDIM))
        s = s - jnp.max(s, axis=-1, keepdims=True)
        p = jnp.exp(s)
        p = p * (1.0 / jnp.sum(p, axis=-1, keepdims=True))
        outs.append(jnp.dot(p.astype(BF16), mv[:, sl], preferred_element_type=F32))
    om = jnp.concatenate(outs, axis=1).astype(BF16)
    x2 = x1 + jnp.dot(om, wmo_ref[...], preferred_element_type=F32)
    xn = _rms_lane(x2, gffn_ref[...]).astype(BF16)
    acc = x2
    d_ff = w1_ref.shape[1]
    for c in range(d_ff // ff_chunk):
        hc = jnp.dot(xn, w1_ref[:, c * ff_chunk:(c + 1) * ff_chunk], preferred_element_type=F32)
        hc = jnp.square(jnp.maximum(hc, 0.0)).astype(BF16)
        acc = acc + jnp.dot(hc, w2_ref[c * ff_chunk:(c + 1) * ff_chunk, :], preferred_element_type=F32)
    o_ref[0] = acc


def _post(x, ya, yb, ycT, mk, mv, P, *, tm):
    B, T, D = x.shape
    nt = T // tm
    M = mk.shape[1]
    consts = [P['w_out'], P['g_mem'], P['w_mq'], P['mq_g'], P['w_mo'], P['g_ffn'], P['w_ff1'], P['w_ff2']]
    tile = lambda w: pl.BlockSpec((1, tm, w), lambda b, t: (b, t, 0))
    return pl.pallas_call(
        functools.partial(_post_kernel, ff_chunk=1024), grid=(B, nt),
        in_specs=[tile(D), tile(A_WIDTH), tile(B_WIDTH),
                  pl.BlockSpec((1, C_HEADS * C_VDIM, tm), lambda b, t: (b, 0, t)),
                  pl.BlockSpec((1, M, MEM_HEADS * MEM_HDIM), lambda b, t: (b, 0, 0)),
                  pl.BlockSpec((1, M, MEM_HEADS * MEM_HDIM), lambda b, t: (b, 0, 0))]
                 + [_const_spec(c.shape) for c in consts],
        out_specs=tile(D),
        out_shape=jax.ShapeDtypeStruct((B, T, D), F32),
        compiler_params=_cparams(2), name=f"post_{tm}",
    )(x, ya, yb, ycT, mk, mv, *consts)


def _memkv_kernel(mem_ref, g_ref, wk_ref, wv_ref, kg_ref, k_ref, v_ref):
    mn = _rms_lane(mem_ref[0], g_ref[...]).astype(BF16)
    k = jnp.dot(mn, wk_ref[...], preferred_element_type=F32)
    v_ref[0] = jnp.dot(mn, wv_ref[...], preferred_element_type=F32)
    for hh in range(MEM_HEADS):
        sl = slice(hh * MEM_HDIM, (hh + 1) * MEM_HDIM)
        k_ref[0, :, sl] = _rms_lane(k[:, sl], kg_ref[...])


def _memkv(mem, P):
    B, M, D = mem.shape
    W = MEM_HEADS * MEM_HDIM
    consts = [P['g_memkv'], P['w_mk'], P['w_mv'], P['mk_g']]
    return pl.pallas_call(
        _memkv_kernel, grid=(B,),
        in_specs=[pl.BlockSpec((1, M, D), lambda b: (b, 0, 0))] + [_const_spec(c.shape) for c in consts],
        out_specs=(pl.BlockSpec((1, M, W), lambda b: (b, 0, 0)), pl.BlockSpec((1, M, W), lambda b: (b, 0, 0))),
        out_shape=(jax.ShapeDtypeStruct((B, M, W), F32), jax.ShapeDtypeStruct((B, M, W), F32)),
        compiler_params=_cparams(1), name="memkv",
    )(mem, *consts)


def _row(v):
    return v.reshape(1, -1).astype(F32)


def _prep_layer(l, W, L):
    P = {}
    P['g_mix'] = _row(W['norm_mix_g'][l])
    P['w_in'] = jnp.pad(W['w_in'][l], ((0, 0), (0, IN_PAD - W['w_in'].shape[2]))).astype(BF16)
    P['a_g'] = _row(W['a_norm_g'][l])
    P['a_ws'] = W['a_ws'][l][:, :L, :L]
    P['a_bias'] = jnp.repeat(W['a_bs'][l][:, :L].T, A_GDIM, axis=1)
    P['dw_w'] = W['b_dw_w'][l]
    P['dw_b'] = _row(W['b_dw_b'][l])
    P['ln_g'] = _row(W['b_ln_g'][l])
    P['ln_b'] = _row(W['b_ln_b'][l])
    P['qa_g'] = _row(W['c_qa_g'][l])
    wq = W['c_w_uq'][l].reshape(C_QRANK, C_HEADS, C_NOPE + C_ROPE)
    wq = jnp.pad(wq, ((0, 0), (0, 0), (0, HEAD_PAD - C_NOPE - C_ROPE)))
    P['w_uqT'] = wq.reshape(C_QRANK, C_HEADS * HEAD_PAD).T.astype(BF16)
    P['qn_g'] = W['c_qn_g'][l].reshape(C_NOPE, 1)
    P['qr_g'] = W['c_qr_g'][l].reshape(C_ROPE, 1)
    P['kva_g'] = _row(W['c_kva_g'][l])
    P['kr_g'] = jnp.pad(_row(W['c_kr_g'][l]), ((0, 0), (0, 128 - C_ROPE)))
    wkv = W['c_w_ukv'][l].reshape(C_KVRANK, C_HEADS, C_NOPE + C_VDIM)
    P['w_uk'] = jnp.pad(wkv[:, :, :C_NOPE], ((0, 0), (0, 0), (0, HEAD_PAD - C_NOPE))
                        ).reshape(C_KVRANK, C_HEADS * HEAD_PAD).astype(BF16)
    P['w_uvT'] = wkv[:, :, C_NOPE:].reshape(C_KVRANK, C_HEADS * C_VDIM).T.astype(BF16)
    P['kn_g'] = jnp.pad(_row(W['c_kn_g'][l]), ((0, 0), (0, HEAD_PAD - C_NOPE)))
    P['w_out'] = W['w_out'][l].astype(BF16)
    P['g_mem'] = _row(W['norm_mem_g'][l])
    P['w_mq'] = W['w_mq'][l].astype(BF16)
    P['mq_g'] = _row(W['m_q_g'][l])
    P['w_mo'] = W['w_mo'][l].astype(BF16)
    P['g_ffn'] = _row(W['norm_ffn_g'][l])
    P['w_ff1'] = W['w_ff1'][l].astype(BF16)
    P['w_ff2'] = W['w_ff2'][l].astype(BF16)
    P['g_memkv'] = _row(W['mem_norm_g'][l])
    P['w_mk'] = W['w_mk'][l].astype(BF16)
    P['w_mv'] = W['w_mv'][l].astype(BF16)
    P['mk_g'] = _row(W['m_k_g'][l])
    return P


def _rope_tables(start, t):
    half = C_ROPE // 2
    inv = ROPE_THETA ** (-jnp.arange(half, dtype=F32) / half)
    ang = (start + jnp.arange(t)).astype(F32)[:, None] * inv[None, :]
    cos, sin = jnp.cos(ang), jnp.sin(ang)
    zeros = jnp.zeros((t, 128 - C_ROPE), F32)
    cosk = jnp.concatenate([cos, cos, zeros], axis=1)
    sink = jnp.concatenate([-sin, sin, zeros], axis=1)
    return cos.T, sin.T, cosk, sink


def _pad_kr(kr):
    return jnp.pad(kr, ((0, 0), (0, 0), (C_NOPE, HEAD_PAD - C_NOPE - C_ROPE)))


def _layer(x, hist, tabs, lat_past, kr_past, mk, mv, P, *, tm, L, tq, start):
    ya, yb, qT, lat, kr, cstate, gv = _inproj(x, hist, tabs, P, tm=tm, L=L)
    if lat_past is None:
        lat_all, kr_all, tk = lat, kr, tq
    else:
        lat_all = jnp.concatenate([lat_past, lat], axis=1)
        kr_all = jnp.concatenate([kr_past, kr], axis=1)
        tk = lat_all.shape[1]
    k, vT = _kvgen(lat_all, _pad_kr(kr_all), P, tk=tk)
    ycT = _attn(qT, k, vT, tq=tq, q_pos0=start, causal_tiles=lat_past is None)
    y = _post(x, ya, yb, ycT, mk, mv, P, tm=tm)
    return y, gv, cstate, lat, kr


def kernel(x_prompt, x_sample, mem_prompt, cache_mla_latent, cache_mla_krope, cache_conv, cache_mem_k, cache_mem_v, norm_mix_g, w_in, a_norm_g, a_ws, a_bs, b_dw_w, b_dw_b, b_ln_g, b_ln_b, c_qa_g, c_w_uq, c_kva_g, c_w_ukv, c_qn_g, c_qr_g, c_kn_g, c_kr_g, w_out, norm_mem_g, mem_norm_g, w_mq, w_mk, w_mv, w_mo, m_q_g, m_k_g, norm_ffn_g, w_ff1, w_ff2):
    W = dict(norm_mix_g=norm_mix_g, w_in=w_in, a_norm_g=a_norm_g, a_ws=a_ws, a_bs=a_bs,
             b_dw_w=b_dw_w, b_dw_b=b_dw_b, b_ln_g=b_ln_g, b_ln_b=b_ln_b, c_qa_g=c_qa_g,
             c_w_uq=c_w_uq, c_kva_g=c_kva_g, c_w_ukv=c_w_ukv, c_qn_g=c_qn_g, c_qr_g=c_qr_g,
             c_kn_g=c_kn_g, c_kr_g=c_kr_g, w_out=w_out, norm_mem_g=norm_mem_g,
             mem_norm_g=mem_norm_g, w_mq=w_mq, w_mk=w_mk, w_mv=w_mv, w_mo=w_mo, m_q_g=m_q_g,
             m_k_g=m_k_g, norm_ffn_g=norm_ffn_g, w_ff1=w_ff1, w_ff2=w_ff2)
    depth = w_in.shape[0]
    Bp, Tp, _ = x_prompt.shape
    Bs, Ts, _ = x_sample.shape
    past = cache_mla_latent.shape[2]
    M = mem_prompt.shape[1]
    MW = MEM_HEADS * MEM_HDIM

    tm_p = min(512, Tp)
    L_p = min(Tp, A_CHUNK)
    L_s = min(Ts, A_CHUNK)
    tabs_p = _rope_tables(0, Tp)
    tabs_s = _rope_tables(past, Ts)
    hist0 = jnp.zeros((Bp, B_KW - 1, B_WIDTH), F32)

    hp, hs = x_prompt, x_sample
    lat_p, kr_p, conv_p, mk_p, mv_p = [], [], [], [], []
    lat_s, kr_s, conv_s, gv_s = [], [], [], []
    for l in range(depth):
        Pp = _prep_layer(l, W, L_p)
        Ps = Pp if L_s == L_p else dict(Pp, a_ws=W['a_ws'][l][:, :L_s, :L_s],
                                        a_bias=jnp.repeat(W['a_bs'][l][:, :L_s].T, A_GDIM, axis=1))
        mk, mv = _memkv(mem_prompt, Pp)
        hp, _, cst, lat, kr = _layer(hp, hist0, tabs_p, None, None, mk, mv, Pp,
                                     tm=tm_p, L=L_p, tq=tm_p, start=0)
        lat_p.append(lat); kr_p.append(kr); conv_p.append(cst)
        mk_p.append(mk.reshape(Bp, M, MEM_HEADS, MEM_HDIM)); mv_p.append(mv.reshape(Bp, M, MEM_HEADS, MEM_HDIM))
        hs, gv, cst, lat, kr = _layer(hs, cache_conv[l], tabs_s, cache_mla_latent[l], cache_mla_krope[l],
                                      cache_mem_k[l].reshape(Bs, M, MW), cache_mem_v[l].reshape(Bs, M, MW), Ps,
                                      tm=Ts, L=L_s, tq=Ts, start=past)
        lat_s.append(lat); kr_s.append(kr); conv_s.append(cst); gv_s.append(gv)
    return (hp, hs,
            jnp.stack(lat_p), jnp.stack(kr_p), jnp.stack(conv_p), jnp.stack(mk_p), jnp.stack(mv_p),
            jnp.stack(lat_s), jnp.stack(kr_s), jnp.stack(conv_s), jnp.stack(gv_s))
```

```python
import functools
import math

import jax
import jax.numpy as jnp
from jax import lax
from jax.experimental import pallas as pl
from jax.experimental.pallas import tpu as pltpu

F32 = jnp.float32
BF16 = jnp.bfloat16

EPS = 1e-6
CHUNK_SHIFT = 6
A_WIDTH = 256
A_GROUPS = 4
A_GDIM = 64
A_CHUNK = 128
B_WIDTH = 256
B_KW = 31
C_HEADS = 8
C_NOPE = 64
C_ROPE = 32
C_VDIM = 64
C_QRANK = 256
C_KVRANK = 128
ROPE_THETA = 10000.0
MEM_HEADS = 4
MEM_HDIM = 128
V_ROWS = C_VDIM + 16
HEAD_PAD = 128
IN_PAD = 1536
HIST_PAD = 32
QK_SCALE = math.log2(math.e) / math.sqrt(C_NOPE + C_ROPE)

ATTN_HEAD_GROUP = 4
VMEM_LIMIT = 56 * 1024 * 1024


def _cparams(n_axes):
    return pltpu.CompilerParams(dimension_semantics=("arbitrary",) * n_axes,
                                vmem_limit_bytes=VMEM_LIMIT)


def _rms_lane(x, g):
    return x * lax.rsqrt(jnp.mean(x * x, axis=-1, keepdims=True) + EPS) * g


def _gelu(x):
    return 0.5 * x * (1.0 + lax.erf(x * (1.0 / math.sqrt(2.0))))


def _const_spec(shape):
    nd = len(shape)
    return pl.BlockSpec(shape, lambda *_: (0,) * nd, pipeline_mode=pl.Buffered(1))


def _inproj_kernel(x_ref, hist_ref, cosq_ref, sinq_ref, cosk_ref, sink_ref,
                   gmix_ref, win_ref, ag_ref, aws_ref, abias_ref,
                   dww_ref, dwb_ref, lng_ref, lnb_ref,
                   qag_ref, wuqT_ref, qng_ref, qrg_ref, kvag_ref, krg_ref,
                   ya_ref, yb_ref, qT_ref, lat_ref, kr_ref, cs_ref, gv_ref,
                   zbuf, *, tm, L):
    t = pl.program_id(1)
    x = x_ref[0]
    xn = _rms_lane(x, gmix_ref[...])
    h = jnp.dot(xn.astype(BF16), win_ref[...], preferred_element_type=F32)
    a_u = h[:, 0:256]
    a_v = h[:, 256:512]
    b_a = h[:, 512:768]
    b_g = h[:, 768:1024]
    c_q = h[:, 1024:1280]
    c_kv = h[:, 1280:1408]
    c_kr = h[:, 1408:1536]

    u = _gelu(a_u)
    v = _rms_lane(_gelu(a_v), ag_ref[...])
    gv_ref[0] = v
    vb = v.astype(BF16)
    row = lax.broadcasted_iota(jnp.int32, (L, L), 0)
    col = lax.broadcasted_iota(jnp.int32, (L, L), 1)
    tril = col <= row
    wcat = jnp.concatenate(
        [jnp.where(tril, aws_ref[g], 0.0).astype(BF16) for g in range(A_GROUPS)], axis=1)
    lane = lax.broadcasted_iota(jnp.int32, (L, A_WIDTH), 1)
    gmasks = [(lane >= g * A_GDIM) & (lane < (g + 1) * A_GDIM) for g in range(A_GROUPS)]
    zero_b = jnp.zeros((L, A_WIDTH), BF16)
    bias = abias_ref[...]
    for c in range(tm // L):
        vc = vb[c * L:(c + 1) * L]
        vstack = jnp.concatenate([jnp.where(gmasks[g], vc, zero_b) for g in range(A_GROUPS)], axis=0)
        mixed = jnp.dot(wcat, vstack, preferred_element_type=F32) + bias
        ya_ref[0, c * L:(c + 1) * L, :] = (u[c * L:(c + 1) * L] * mixed).astype(ya_ref.dtype)

    z = b_a * jax.nn.sigmoid(b_g)

    @pl.when(t == 0)
    def _():
        zbuf[0:8, :] = jnp.zeros((8, B_WIDTH), F32)
        zbuf[HIST_PAD - (B_KW - 1):HIST_PAD, :] = hist_ref[0]

    @pl.when(t > 0)
    def _():
        zbuf[0:HIST_PAD, :] = zbuf[tm:tm + HIST_PAD, :]

    zbuf[HIST_PAD:HIST_PAD + tm, :] = z
    off = HIST_PAD - (B_KW - 1)
    acc = jnp.zeros((tm, B_WIDTH), F32) + dwb_ref[...]
    for j in range(B_KW):
        acc = acc + dww_ref[j:j + 1, :] * zbuf[off + j:off + j + tm, :]
    cs_ref[0] = zbuf[tm + off:tm + HIST_PAD, :]
    mu = jnp.mean(acc, axis=-1, keepdims=True)
    xc = acc - mu
    yn = xc * lax.rsqrt(jnp.mean(xc * xc, axis=-1, keepdims=True) + EPS) * lng_ref[...] + lnb_ref[...]
    yb_ref[0] = (yn * jax.nn.sigmoid(yn)).astype(yb_ref.dtype)

    qa = _rms_lane(c_q, qag_ref[...]).astype(BF16)
    qT = lax.dot_general(wuqT_ref[...], qa, (((1,), (1,)), ((), ())),
                         preferred_element_type=F32)
    cq = cosq_ref[...]
    sq = sinq_ref[...]
    gqn = qng_ref[...] * QK_SCALE
    gqr = qrg_ref[...]
    half = C_ROPE // 2
    zpad = jnp.zeros((HEAD_PAD - C_NOPE - C_ROPE, tm), F32)
    for hh in range(C_HEADS):
        blk = qT[hh * HEAD_PAD:(hh + 1) * HEAD_PAD]
        qn = blk[0:C_NOPE]
        qr = blk[C_NOPE:C_NOPE + C_ROPE]
        qn = qn * lax.rsqrt(jnp.mean(qn * qn, axis=0, keepdims=True) + EPS) * gqn
        qr = qr * lax.rsqrt(jnp.mean(qr * qr, axis=0, keepdims=True) + EPS) * gqr
        x1 = qr[0:half]
        x2 = qr[half:C_ROPE]
        r1 = (x1 * cq - x2 * sq) * QK_SCALE
        r2 = (x1 * sq + x2 * cq) * QK_SCALE
        qT_ref[0, hh] = jnp.concatenate([qn, r1, r2, zpad], axis=0).astype(qT_ref.dtype)

    lat_ref[0] = _rms_lane(c_kv, kvag_ref[...])
    krn = c_kr * lax.rsqrt(jnp.sum(c_kr * c_kr, axis=-1, keepdims=True) * (1.0 / C_ROPE) + EPS) * krg_ref[...]
    lane_k = lax.broadcasted_iota(jnp.int32, krn.shape, 1)
    partner = jnp.where(lane_k < half, pltpu.roll(krn, 128 - half, 1), pltpu.roll(krn, half, 1))
    kro = krn * cosk_ref[...] + partner * sink_ref[...]
    kr_ref[0] = kro[:, 0:C_ROPE]


def _inproj(x, hist, tabs, P, *, tm, L):
    B, T, D = x.shape
    nt = T // tm
    cosq, sinq, cosk, sink = tabs
    kern = functools.partial(_inproj_kernel, tm=tm, L=L)
    tile = lambda w: pl.BlockSpec((1, tm, w), lambda b, t: (b, t, 0))
    in_specs = [
        tile(D),
        pl.BlockSpec((1, B_KW - 1, B_WIDTH), lambda b, t: (b, 0, 0)),
        pl.BlockSpec((C_ROPE // 2, tm), lambda b, t: (0, t)),
        pl.BlockSpec((C_ROPE // 2, tm), lambda b, t: (0, t)),
        pl.BlockSpec((tm, 128), lambda b, t: (t, 0)),
        pl.BlockSpec((tm, 128), lambda b, t: (t, 0)),
    ]
    consts = [P['g_mix'], P['w_in'], P['a_g'], P['a_ws'], P['a_bias'],
              P['dw_w'], P['dw_b'], P['ln_g'], P['ln_b'],
              P['qa_g'], P['w_uqT'], P['qn_g'], P['qr_g'], P['kva_g'], P['kr_g']]
    in_specs += [_const_spec(c.shape) for c in consts]
    out_shape = (
        jax.ShapeDtypeStruct((B, T, A_WIDTH), BF16),
        jax.ShapeDtypeStruct((B, T, B_WIDTH), BF16),
        jax.ShapeDtypeStruct((B, C_HEADS, HEAD_PAD, T), BF16),
        jax.ShapeDtypeStruct((B, T, C_KVRANK), F32),
        jax.ShapeDtypeStruct((B, T, C_ROPE), F32),
        jax.ShapeDtypeStruct((B, B_KW - 1, B_WIDTH), F32),
        jax.ShapeDtypeStruct((B, T, A_WIDTH), F32),
    )
    out_specs = (
        tile(A_WIDTH), tile(B_WIDTH),
        pl.BlockSpec((1, C_HEADS, HEAD_PAD, tm), lambda b, t: (b, 0, 0, t)),
        tile(C_KVRANK), tile(C_ROPE),
        pl.BlockSpec((1, B_KW - 1, B_WIDTH), lambda b, t: (b, 0, 0)),
        tile(A_WIDTH),
    )
    return pl.pallas_call(
        kern, grid=(B, nt), in_specs=in_specs, out_specs=out_specs, out_shape=out_shape,
        scratch_shapes=[pltpu.VMEM((tm + HIST_PAD, B_WIDTH), F32)],
        compiler_params=_cparams(2), name=f"inproj_{tm}",
    )(x, hist, cosq, sinq, cosk, sink, *consts)


def _kvgen_kernel(lat_ref, krp_ref, wuk_ref, kng_ref, wuvT_ref, k_ref, vT_ref):
    latb = lat_ref[0].astype(BF16)
    kf = jnp.dot(latb, wuk_ref[...], preferred_element_type=F32)
    krp = krp_ref[0]
    g = kng_ref[...]
    vT = lax.dot_general(wuvT_ref[...], latb, (((1,), (1,)), ((), ())),
                         preferred_element_type=F32)
    for hh in range(C_HEADS):
        kh = kf[:, hh * HEAD_PAD:(hh + 1) * HEAD_PAD]
        ms = jnp.sum(kh * kh, axis=-1, keepdims=True) * (1.0 / C_NOPE)
        k_ref[0, hh, 0] = (kh * lax.rsqrt(ms + EPS) * g + krp).astype(k_ref.dtype)
        vT_ref[0, hh, 0, 0:C_VDIM, :] = vT[hh * C_VDIM:(hh + 1) * C_VDIM].astype(vT_ref.dtype)
        vT_ref[0, hh, 0, C_VDIM:V_ROWS, :] = jnp.ones((V_ROWS - C_VDIM, vT.shape[1]), vT_ref.dtype)


def _kvgen(lat, krp, P, *, tk):
    B, Tk, _ = lat.shape
    nkt = Tk // tk
    consts = [P['w_uk'], P['kn_g'], P['w_uvT']]
    return pl.pallas_call(
        _kvgen_kernel, grid=(B, nkt),
        in_specs=[pl.BlockSpec((1, tk, C_KVRANK), lambda b, t: (b, t, 0)),
                  pl.BlockSpec((1, tk, HEAD_PAD), lambda b, t: (b, t, 0))]
                 + [_const_spec(c.shape) for c in consts],
        out_specs=(pl.BlockSpec((1, C_HEADS, 1, tk, HEAD_PAD), lambda b, t: (b, 0, t, 0, 0)),
                   pl.BlockSpec((1, C_HEADS, 1, V_ROWS, tk), lambda b, t: (b, 0, t, 0, 0))),
        out_shape=(jax.ShapeDtypeStruct((B, C_HEADS, nkt, tk, HEAD_PAD), BF16),
                   jax.ShapeDtypeStruct((B, C_HEADS, nkt, V_ROWS, tk), BF16)),
        compiler_params=_cparams(2), name=f"kvgen_{tk}",
    )(lat, krp, *consts)


def _attn_kernel(qT_ref, k_ref, vT_ref, o_ref, *bufs, tq, tk, q_pos0, hg):
    qi = pl.program_id(2)

    def scores(hh, kt, masked):
        s = jnp.dot(k_ref[0, hh, kt], qT_ref[0, hh], preferred_element_type=F32)
        if masked:
            kpos = kt * tk + lax.broadcasted_iota(jnp.int32, (tk, 1), 0)
            qpos = q_pos0 + qi * tq + lax.broadcasted_iota(jnp.int32, (1, tq), 1)
            vis = lax.shift_right_logical(kpos, CHUNK_SHIFT) <= lax.shift_right_logical(qpos, CHUNK_SHIFT)
            s = jnp.where(vis, s, -1e30)
        bufs[hh][...] = s
        return jnp.max(s, axis=0, keepdims=True)

    def update(hh, kt, cmax, state):
        m, acc = state
        m_new = jnp.maximum(m, cmax)
        alpha = jnp.exp2(m - m_new)
        p = jnp.exp2(bufs[hh][...] - m_new)
        pv = jnp.dot(vT_ref[0, hh, kt], p.astype(BF16), preferred_element_type=F32)
        return m_new, alpha * acc + pv

    state0 = (jnp.full((1, tq), -1e30, F32), jnp.zeros((V_ROWS, tq), F32))
    last = hg - 1

    def sweep(kt, masked, c_pend, kt_pend, states):
        states = list(states)
        for hh in range(hg):
            c_new = scores(hh, kt, masked)
            if hh == 0:
                if c_pend is not None:
                    states[last] = update(last, kt_pend, c_pend, states[last])
            else:
                states[hh - 1] = update(hh - 1, kt, c_prev, states[hh - 1])
            c_prev = c_new
        return c_prev, tuple(states)

    c_pend, states = sweep(qi, True, None, None, (state0,) * hg)

    def body(kt, carry):
        c_pend, kt_pend, states = carry
        c_pend, states = sweep(kt, False, c_pend, kt_pend, states)
        return c_pend, kt, states

    c_pend, kt_pend, states = lax.fori_loop(0, qi, body, (c_pend, qi, states))
    states = states[:last] + (update(last, kt_pend, c_pend, states[last]),)
    for hh, (_, acc) in enumerate(states):
        inv_l = 1.0 / acc[C_VDIM:C_VDIM + 1]
        o_ref[0, hh * C_VDIM:(hh + 1) * C_VDIM, :] = (acc[0:C_VDIM] * inv_l).astype(o_ref.dtype)


def _attn(qT, k, vT, *, tq, q_pos0):
    B, H, _, Tq = qT.shape
    _, _, nkt, tk, _ = k.shape
    nq = Tq // tq
    hg = ATTN_HEAD_GROUP
    kern = functools.partial(_attn_kernel, tq=tq, tk=tk, q_pos0=q_pos0, hg=hg)
    return pl.pallas_call(
        kern, grid=(B, H // hg, nq),
        in_specs=[pl.BlockSpec((1, hg, HEAD_PAD, tq), lambda b, h, q: (b, h, 0, q)),
                  pl.BlockSpec((1, hg, nkt, tk, HEAD_PAD), lambda b, h, q: (b, h, 0, 0, 0)),
                  pl.BlockSpec((1, hg, nkt, V_ROWS, tk), lambda b, h, q: (b, h, 0, 0, 0))],
        out_specs=pl.BlockSpec((1, hg * C_VDIM, tq), lambda b, h, q: (b, h, q)),
        out_shape=jax.ShapeDtypeStruct((B, H * C_VDIM, Tq), BF16),
        scratch_shapes=[pltpu.VMEM((tk, tq), F32) for _ in range(hg)],
        compiler_params=_cparams(3), name=f"attn_{tq}",
    )(qT, k, vT)


def _post_kernel(x_ref, ya_ref, yb_ref, ycT_ref, mk_ref, mv_ref,
                 wout_ref, gmem_ref, wmq_ref, mqg_ref, wmo_ref, gffn_ref, w1_ref, w2_ref,
                 o_ref, *, ff_chunk):
    x = x_ref[0]
    ab = A_WIDTH + B_WIDTH
    mix_ab = jnp.concatenate([ya_ref[0], yb_ref[0]], axis=1)
    x1 = x + jnp.dot(mix_ab, wout_ref[0:ab, :], preferred_element_type=F32)
    x1 = x1 + lax.dot_general(ycT_ref[0], wout_ref[ab:, :], (((0,), (0,)), ((), ())),
                              preferred_element_type=F32)
    qm = jnp.dot(_rms_lane(x1, gmem_ref[...]).astype(BF16), wmq_ref[...], preferred_element_type=F32)
    mk = mk_ref[0].astype(BF16)
    mv = mv_ref[0].astype(BF16)
    outs = []
    for hh in range(MEM_HEADS):
        sl = slice(hh * MEM_HDIM, (hh + 1) * MEM_HDIM)
        qh = _rms_lane(qm[:, sl], mqg_ref[...]).astype(BF16)
        s = lax.dot_general(qh, mk[:, sl], (((1,), (1,)), ((), ())),
                            preferred_element_type=F32) * (1.0 / math.sqrt(MEM_HDIM))
        s = s - jnp.max(s, axis=-1, keepdims=True)
        p = jnp.exp(s)
        p = p * (1.0 / jnp.sum(p, axis=-1, keepdims=True))
        outs.append(jnp.dot(p.astype(BF16), mv[:, sl], preferred_element_type=F32))
    om = jnp.concatenate(outs, axis=1).astype(BF16)
    x2 = x1 + jnp.dot(om, wmo_ref[...], preferred_element_type=F32)
    xn = _rms_lane(x2, gffn_ref[...]).astype(BF16)
    acc = x2
    d_ff = w1_ref.shape[1]
    for c in range(d_ff // ff_chunk):
        hc = jnp.dot(xn, w1_ref[:, c * ff_chunk:(c + 1) * ff_chunk], preferred_element_type=F32)
        hc = jnp.square(jnp.maximum(hc, 0.0)).astype(BF16)
        acc = acc + jnp.dot(hc, w2_ref[c * ff_chunk:(c + 1) * ff_chunk, :], preferred_element_type=F32)
    o_ref[0] = acc


def _post(x, ya, yb, ycT, mk, mv, P, *, tm):
    B, T, D = x.shape
    nt = T // tm
    M = mk.shape[1]
    consts = [P['w_out'], P['g_mem'], P['w_mq'], P['mq_g'], P['w_mo'], P['g_ffn'], P['w_ff1'], P['w_ff2']]
    tile = lambda w: pl.BlockSpec((1, tm, w), lambda b, t: (b, t, 0))
    return pl.pallas_call(
        functools.partial(_post_kernel, ff_chunk=1024), grid=(B, nt),
        in_specs=[tile(D), tile(A_WIDTH), tile(B_WIDTH),
                  pl.BlockSpec((1, C_HEADS * C_VDIM, tm), lambda b, t: (b, 0, t)),
                  pl.BlockSpec((1, M, MEM_HEADS * MEM_HDIM), lambda b, t: (b, 0, 0)),
                  pl.BlockSpec((1, M, MEM_HEADS * MEM_HDIM), lambda b, t: (b, 0, 0))]
                 + [_const_spec(c.shape) for c in consts],
        out_specs=tile(D),
        out_shape=jax.ShapeDtypeStruct((B, T, D), F32),
        compiler_params=_cparams(2), name=f"post_{tm}",
    )(x, ya, yb, ycT, mk, mv, *consts)


def _memkv_kernel(mem_ref, g_ref, wk_ref, wv_ref, kg_ref, k_ref, v_ref):
    mn = _rms_lane(mem_ref[0], g_ref[...]).astype(BF16)
    k = jnp.dot(mn, wk_ref[...], preferred_element_type=F32)
    v_ref[0] = jnp.dot(mn, wv_ref[...], preferred_element_type=F32)
    for hh in range(MEM_HEADS):
        sl = slice(hh * MEM_HDIM, (hh + 1) * MEM_HDIM)
        k_ref[0, :, sl] = _rms_lane(k[:, sl], kg_ref[...])


def _memkv(mem, P):
    B, M, D = mem.shape
    W = MEM_HEADS * MEM_HDIM
    consts = [P['g_memkv'], P['w_mk'], P['w_mv'], P['mk_g']]
    return pl.pallas_call(
        _memkv_kernel, grid=(B,),
        in_specs=[pl.BlockSpec((1, M, D), lambda b: (b, 0, 0))] + [_const_spec(c.shape) for c in consts],
        out_specs=(pl.BlockSpec((1, M, W), lambda b: (b, 0, 0)), pl.BlockSpec((1, M, W), lambda b: (b, 0, 0))),
        out_shape=(jax.ShapeDtypeStruct((B, M, W), F32), jax.ShapeDtypeStruct((B, M, W), F32)),
        compiler_params=_cparams(1), name="memkv",
    )(mem, *consts)


def _row(v):
    return v.reshape(1, -1).astype(F32)


def _prep_layer(l, W, L):
    P = {}
    P['g_mix'] = _row(W['norm_mix_g'][l])
    P['w_in'] = jnp.pad(W['w_in'][l], ((0, 0), (0, IN_PAD - W['w_in'].shape[2]))).astype(BF16)
    P['a_g'] = _row(W['a_norm_g'][l])
    P['a_ws'] = W['a_ws'][l][:, :L, :L]
    P['a_bias'] = jnp.repeat(W['a_bs'][l][:, :L].T, A_GDIM, axis=1)
    P['dw_w'] = W['b_dw_w'][l]
    P['dw_b'] = _row(W['b_dw_b'][l])
    P['ln_g'] = _row(W['b_ln_g'][l])
    P['ln_b'] = _row(W['b_ln_b'][l])
    P['qa_g'] = _row(W['c_qa_g'][l])
    wq = W['c_w_uq'][l].reshape(C_QRANK, C_HEADS, C_NOPE + C_ROPE)
    wq = jnp.pad(wq, ((0, 0), (0, 0), (0, HEAD_PAD - C_NOPE - C_ROPE)))
    P['w_uqT'] = wq.reshape(C_QRANK, C_HEADS * HEAD_PAD).T.astype(BF16)
    P['qn_g'] = W['c_qn_g'][l].reshape(C_NOPE, 1)
    P['qr_g'] = W['c_qr_g'][l].reshape(C_ROPE, 1)
    P['kva_g'] = _row(W['c_kva_g'][l])
    P['kr_g'] = jnp.pad(_row(W['c_kr_g'][l]), ((0, 0), (0, 128 - C_ROPE)))
    wkv = W['c_w_ukv'][l].reshape(C_KVRANK, C_HEADS, C_NOPE + C_VDIM)
    P['w_uk'] = jnp.pad(wkv[:, :, :C_NOPE], ((0, 0), (0, 0), (0, HEAD_PAD - C_NOPE))
                        ).reshape(C_KVRANK, C_HEADS * HEAD_PAD).astype(BF16)
    P['w_uvT'] = wkv[:, :, C_NOPE:].reshape(C_KVRANK, C_HEADS * C_VDIM).T.astype(BF16)
    P['kn_g'] = jnp.pad(_row(W['c_kn_g'][l]), ((0, 0), (0, HEAD_PAD - C_NOPE)))
    P['w_out'] = W['w_out'][l].astype(BF16)
    P['g_mem'] = _row(W['norm_mem_g'][l])
    P['w_mq'] = W['w_mq'][l].astype(BF16)
    P['mq_g'] = _row(W['m_q_g'][l])
    P['w_mo'] = W['w_mo'][l].astype(BF16)
    P['g_ffn'] = _row(W['norm_ffn_g'][l])
    P['w_ff1'] = W['w_ff1'][l].astype(BF16)
    P['w_ff2'] = W['w_ff2'][l].astype(BF16)
    P['g_memkv'] = _row(W['mem_norm_g'][l])
    P['w_mk'] = W['w_mk'][l].astype(BF16)
    P['w_mv'] = W['w_mv'][l].astype(BF16)
    P['mk_g'] = _row(W['m_k_g'][l])
    return P


def _rope_tables(start, t):
    half = C_ROPE // 2
    inv = ROPE_THETA ** (-jnp.arange(half, dtype=F32) / half)
    ang = (start + jnp.arange(t)).astype(F32)[:, None] * inv[None, :]
    cos, sin = jnp.cos(ang), jnp.sin(ang)
    zeros = jnp.zeros((t, 128 - C_ROPE), F32)
    cosk = jnp.concatenate([cos, cos, zeros], axis=1)
    sink = jnp.concatenate([-sin, sin, zeros], axis=1)
    return cos.T, sin.T, cosk, sink


def _pad_kr(kr):
    return jnp.pad(kr, ((0, 0), (0, 0), (C_NOPE, HEAD_PAD - C_NOPE - C_ROPE)))


def _layer(x, hist, tabs, lat_past, kr_past, mk, mv, P, *, tm, L, tq, start):
    ya, yb, qT, lat, kr, cstate, gv = _inproj(x, hist, tabs, P, tm=tm, L=L)
    if lat_past is None:
        lat_all, kr_all, tk = lat, kr, tq
    else:
        lat_all = jnp.concatenate([lat_past, lat], axis=1)
        kr_all = jnp.concatenate([kr_past, kr], axis=1)
        tk = lat_all.shape[1]
    k, vT = _kvgen(lat_all, _pad_kr(kr_all), P, tk=tk)
    ycT = _attn(qT, k, vT, tq=tq, q_pos0=start)
    y = _post(x, ya, yb, ycT, mk, mv, P, tm=tm)
    return y, gv, cstate, lat, kr


def kernel(x_prompt, x_sample, mem_prompt, cache_mla_latent, cache_mla_krope, cache_conv, cache_mem_k, cache_mem_v, norm_mix_g, w_in, a_norm_g, a_ws, a_bs, b_dw_w, b_dw_b, b_ln_g, b_ln_b, c_qa_g, c_w_uq, c_kva_g, c_w_ukv, c_qn_g, c_qr_g, c_kn_g, c_kr_g, w_out, norm_mem_g, mem_norm_g, w_mq, w_mk, w_mv, w_mo, m_q_g, m_k_g, norm_ffn_g, w_ff1, w_ff2):
    W = dict(norm_mix_g=norm_mix_g, w_in=w_in, a_norm_g=a_norm_g, a_ws=a_ws, a_bs=a_bs,
             b_dw_w=b_dw_w, b_dw_b=b_dw_b, b_ln_g=b_ln_g, b_ln_b=b_ln_b, c_qa_g=c_qa_g,
             c_w_uq=c_w_uq, c_kva_g=c_kva_g, c_w_ukv=c_w_ukv, c_qn_g=c_qn_g, c_qr_g=c_qr_g,
             c_kn_g=c_kn_g, c_kr_g=c_kr_g, w_out=w_out, norm_mem_g=norm_mem_g,
             mem_norm_g=mem_norm_g, w_mq=w_mq, w_mk=w_mk, w_mv=w_mv, w_mo=w_mo, m_q_g=m_q_g,
             m_k_g=m_k_g, norm_ffn_g=norm_ffn_g, w_ff1=w_ff1, w_ff2=w_ff2)
    depth = w_in.shape[0]
    Bp, Tp, _ = x_prompt.shape
    Bs, Ts, _ = x_sample.shape
    past = cache_mla_latent.shape[2]
    M = mem_prompt.shape[1]
    MW = MEM_HEADS * MEM_HDIM

    tm_p = min(512, Tp)
    L_p = min(Tp, A_CHUNK)
    L_s = min(Ts, A_CHUNK)
    tabs_p = _rope_tables(0, Tp)
    tabs_s = _rope_tables(past, Ts)
    hist0 = jnp.zeros((Bp, B_KW - 1, B_WIDTH), F32)

    hp, hs = x_prompt, x_sample
    lat_p, kr_p, conv_p, mk_p, mv_p = [], [], [], [], []
    lat_s, kr_s, conv_s, gv_s = [], [], [], []
    for l in range(depth):
        Pp = _prep_layer(l, W, L_p)
        Ps = Pp if L_s == L_p else dict(Pp, a_ws=W['a_ws'][l][:, :L_s, :L_s],
                                        a_bias=jnp.repeat(W['a_bs'][l][:, :L_s].T, A_GDIM, axis=1))
        mk, mv = _memkv(mem_prompt, Pp)
        hp, _, cst, lat, kr = _layer(hp, hist0, tabs_p, None, None, mk, mv, Pp,
                                     tm=tm_p, L=L_p, tq=tm_p, start=0)
        lat_p.append(lat); kr_p.append(kr); conv_p.append(cst)
        mk_p.append(mk.reshape(Bp, M, MEM_HEADS, MEM_HDIM)); mv_p.append(mv.reshape(Bp, M, MEM_HEADS, MEM_HDIM))
        hs, gv, cst, lat, kr = _layer(hs, cache_conv[l], tabs_s, cache_mla_latent[l], cache_mla_krope[l],
                                      cache_mem_k[l].reshape(Bs, M, MW), cache_mem_v[l].reshape(Bs, M, MW), Ps,
                                      tm=Ts, L=L_s, tq=Ts, start=past)
        lat_s.append(lat); kr_s.append(kr); conv_s.append(cst); gv_s.append(gv)
    return (hp, hs,
            jnp.stack(lat_p), jnp.stack(kr_p), jnp.stack(conv_p), jnp.stack(mk_p), jnp.stack(mv_p),
            jnp.stack(lat_s), jnp.stack(kr_s), jnp.stack(conv_s), jnp.stack(gv_s))
```

```python
import functools
import math

import jax
import jax.numpy as jnp
from jax import lax
from jax.experimental import pallas as pl
from jax.experimental.pallas import tpu as pltpu

F32 = jnp.float32
BF16 = jnp.bfloat16

EPS = 1e-6
CHUNK_SHIFT = 6
A_WIDTH = 256
A_GROUPS = 4
A_GDIM = 64
A_CHUNK = 128
B_WIDTH = 256
B_KW = 31
C_HEADS = 8
C_NOPE = 64
C_ROPE = 32
C_VDIM = 64
C_QRANK = 256
C_KVRANK = 128
ROPE_THETA = 10000.0
MEM_HEADS = 4
MEM_HDIM = 128
V_ROWS = C_VDIM + 16
HEAD_PAD = 128
IN_PAD = 1536
SUBLANES = 8
HIST_PAD = 32
QK_SCALE = math.log2(math.e) / math.sqrt(C_NOPE + C_ROPE)

ATTN_HEAD_GROUP = 8
VMEM_LIMIT = 56 * 1024 * 1024


def _cparams(n_axes):
    return pltpu.CompilerParams(dimension_semantics=("arbitrary",) * n_axes,
                                vmem_limit_bytes=VMEM_LIMIT)


def _rms_lane(x, g):
    return x * lax.rsqrt(jnp.mean(x * x, axis=-1, keepdims=True) + EPS) * g


def _gelu(x):
    return 0.5 * x * (1.0 + lax.erf(x * (1.0 / math.sqrt(2.0))))


def _const_operands(items):
    arrays, specs = [], []
    for it in items:
        if isinstance(it, tuple):
            arr, layer = it
            nd = arr.ndim - 1
            spec = pl.BlockSpec((None,) + arr.shape[1:], lambda *_, layer=layer, nd=nd: (layer,) + (0,) * nd,
                                pipeline_mode=pl.Buffered(1))
        else:
            arr = it
            spec = pl.BlockSpec(arr.shape, lambda *_, nd=arr.ndim: (0,) * nd, pipeline_mode=pl.Buffered(1))
        arrays.append(arr)
        specs.append(spec)
    return arrays, specs


N_INPROJ_IN = 21


def _emit_kv(lat, krp, wuk_ref, kng_ref, wuvT_ref, k_ref, vT_ref):
    latb = lat.astype(BF16)
    kf = jnp.dot(latb, wuk_ref[...], preferred_element_type=F32)
    g = kng_ref[...]
    vT = lax.dot_general(wuvT_ref[...], latb, (((1,), (1,)), ((), ())),
                         preferred_element_type=F32)
    ones = jnp.ones((V_ROWS - C_VDIM, vT.shape[1]), vT_ref.dtype)
    for hh in range(C_HEADS):
        kh = kf[:, hh * HEAD_PAD:(hh + 1) * HEAD_PAD]
        ms = jnp.sum(kh * kh, axis=-1, keepdims=True) * (1.0 / C_NOPE)
        k_ref[0, hh, 0] = (kh * lax.rsqrt(ms + EPS) * g + krp).astype(k_ref.dtype)
        vT_ref[0, hh, 0, 0:C_VDIM, :] = vT[hh * C_VDIM:(hh + 1) * C_VDIM].astype(vT_ref.dtype)
        vT_ref[0, hh, 0, C_VDIM:V_ROWS, :] = ones


def _inproj_kernel(*refs, tm, L, prompt):
    (x_ref, hist_ref, cosq_ref, sinq_ref, cosk_ref, sink_ref,
     gmix_ref, win_ref, ag_ref, aws_ref, abias_ref,
     dww_ref, dwb_ref, lng_ref, lnb_ref,
     qag_ref, wuqT_ref, qng_ref, qrg_ref, kvag_ref, krg_ref) = refs[:N_INPROJ_IN]
    n_in = N_INPROJ_IN + (3 if prompt else 0)
    ya_ref, yb_ref, qT_ref, lat_ref, kr_ref, cs_ref = refs[n_in:n_in + 6]
    zbuf = refs[-1]
    t = pl.program_id(1)
    x = x_ref[0]
    xn = _rms_lane(x, gmix_ref[...])
    h = jnp.dot(xn.astype(BF16), win_ref[...], preferred_element_type=F32)
    a_u = h[:, 0:256]
    a_v = h[:, 256:512]
    b_a = h[:, 512:768]
    b_g = h[:, 768:1024]
    c_q = h[:, 1024:1280]
    c_kv = h[:, 1280:1408]
    c_kr = h[:, 1408:1536]

    u = _gelu(a_u)
    v = _rms_lane(_gelu(a_v), ag_ref[...])
    if not prompt:
        refs[n_in + 6][0] = v
    vb = v.astype(BF16)
    row = lax.broadcasted_iota(jnp.int32, (L, L), 0)
    col = lax.broadcasted_iota(jnp.int32, (L, L), 1)
    tril = col <= row
    wcat = jnp.concatenate(
        [jnp.where(tril, aws_ref[g], 0.0).astype(BF16) for g in range(A_GROUPS)], axis=1)
    lane = lax.broadcasted_iota(jnp.int32, (L, A_WIDTH), 1)
    gmasks = [(lane >= g * A_GDIM) & (lane < (g + 1) * A_GDIM) for g in range(A_GROUPS)]
    zero_b = jnp.zeros((L, A_WIDTH), BF16)
    bias = abias_ref[...]
    for c in range(tm // L):
        vc = vb[c * L:(c + 1) * L]
        vstack = jnp.concatenate([jnp.where(gmasks[g], vc, zero_b) for g in range(A_GROUPS)], axis=0)
        mixed = jnp.dot(wcat, vstack, preferred_element_type=F32) + bias
        ya_ref[0, c * L:(c + 1) * L, :] = (u[c * L:(c + 1) * L] * mixed).astype(ya_ref.dtype)

    z = b_a * jax.nn.sigmoid(b_g)

    @pl.when(t == 0)
    def _():
        zbuf[0:8, :] = jnp.zeros((8, B_WIDTH), F32)
        zbuf[HIST_PAD - (B_KW - 1):HIST_PAD, :] = hist_ref[0]

    @pl.when(t > 0)
    def _():
        zbuf[0:HIST_PAD, :] = zbuf[tm:tm + HIST_PAD, :]

    zbuf[HIST_PAD:HIST_PAD + tm, :] = z
    zbuf[HIST_PAD + tm:HIST_PAD + tm + SUBLANES, :] = jnp.zeros((SUBLANES, B_WIDTH), F32)
    off = HIST_PAD - (B_KW - 1)
    n_win = tm + SUBLANES
    acc = None
    for a in range(SUBLANES):
        va = None
        for blk in range(HIST_PAD // SUBLANES + 1):
            j = SUBLANES * blk + a - off
            if 0 <= j < B_KW:
                term = dww_ref[j:j + 1, :] * zbuf[SUBLANES * blk:SUBLANES * blk + n_win, :]
                va = term if va is None else va + term
        if a:
            va = pltpu.roll(va, n_win - a, 0)
        acc = va[0:tm] if acc is None else acc + va[0:tm]
    acc = acc + dwb_ref[...]
    cs_ref[0] = zbuf[tm + off:tm + HIST_PAD, :]
    mu = jnp.mean(acc, axis=-1, keepdims=True)
    xc = acc - mu
    yn = xc * lax.rsqrt(jnp.mean(xc * xc, axis=-1, keepdims=True) + EPS) * lng_ref[...] + lnb_ref[...]
    yb_ref[0] = (yn * jax.nn.sigmoid(yn)).astype(yb_ref.dtype)

    qa = _rms_lane(c_q, qag_ref[...]).astype(BF16)
    qT = lax.dot_general(wuqT_ref[...], qa, (((1,), (1,)), ((), ())),
                         preferred_element_type=F32)
    cq = cosq_ref[...]
    sq = sinq_ref[...]
    gqn = qng_ref[...] * QK_SCALE
    gqr = qrg_ref[...]
    half = C_ROPE // 2
    zpad = jnp.zeros((HEAD_PAD - C_NOPE - C_ROPE, tm), F32)
    for hh in range(C_HEADS):
        blk = qT[hh * HEAD_PAD:(hh + 1) * HEAD_PAD]
        qn = blk[0:C_NOPE]
        qr = blk[C_NOPE:C_NOPE + C_ROPE]
        qn = qn * lax.rsqrt(jnp.mean(qn * qn, axis=0, keepdims=True) + EPS) * gqn
        qr = qr * lax.rsqrt(jnp.mean(qr * qr, axis=0, keepdims=True) + EPS) * gqr
        x1 = qr[0:half]
        x2 = qr[half:C_ROPE]
        r1 = (x1 * cq - x2 * sq) * QK_SCALE
        r2 = (x1 * sq + x2 * cq) * QK_SCALE
        qT_ref[0, hh] = jnp.concatenate([qn, r1, r2, zpad], axis=0).astype(qT_ref.dtype)

    lat = _rms_lane(c_kv, kvag_ref[...])
    lat_ref[0] = lat
    krn = c_kr * lax.rsqrt(jnp.sum(c_kr * c_kr, axis=-1, keepdims=True) * (1.0 / C_ROPE) + EPS) * krg_ref[...]
    lane_k = lax.broadcasted_iota(jnp.int32, krn.shape, 1)
    partner = jnp.where(lane_k < half, pltpu.roll(krn, 128 - half, 1), pltpu.roll(krn, half, 1))
    kro = krn * cosk_ref[...] + partner * sink_ref[...]
    kr_ref[0] = kro[:, 0:C_ROPE]
    if prompt:
        wuk_ref, kng_ref, wuvT_ref = refs[N_INPROJ_IN:n_in]
        k_ref, vT_ref = refs[n_in + 6:n_in + 8]
        _emit_kv(lat, pltpu.roll(kro, C_NOPE, 1), wuk_ref, kng_ref, wuvT_ref, k_ref, vT_ref)


def _inproj(x, hist, tabs, P, *, tm, L, prompt):
    B, T, D = x.shape
    nt = T // tm
    cosq, sinq, cosk, sink = tabs
    kern = functools.partial(_inproj_kernel, tm=tm, L=L, prompt=prompt)
    tile = lambda w: pl.BlockSpec((1, tm, w), lambda b, t: (b, t, 0))
    in_specs = [
        tile(D),
        pl.BlockSpec((1, B_KW - 1, B_WIDTH), lambda b, t: (b, 0, 0)),
        pl.BlockSpec((C_ROPE // 2, tm), lambda b, t: (0, t)),
        pl.BlockSpec((C_ROPE // 2, tm), lambda b, t: (0, t)),
        pl.BlockSpec((tm, 128), lambda b, t: (t, 0)),
        pl.BlockSpec((tm, 128), lambda b, t: (t, 0)),
    ]
    consts = [P['g_mix'], P['w_in'], P['a_g'], P['a_ws'], P['a_bias'],
              P['dw_w'], P['dw_b'], P['ln_g'], P['ln_b'],
              P['qa_g'], P['w_uqT'], P['qn_g'], P['qr_g'], P['kva_g'], P['kr_g']]
    if prompt:
        consts += [P['w_uk'], P['kn_g'], P['w_uvT']]
    consts, const_specs = _const_operands(consts)
    in_specs += const_specs
    out_shape = [
        jax.ShapeDtypeStruct((B, T, A_WIDTH), BF16),
        jax.ShapeDtypeStruct((B, T, B_WIDTH), BF16),
        jax.ShapeDtypeStruct((B, C_HEADS, HEAD_PAD, T), BF16),
        jax.ShapeDtypeStruct((B, T, C_KVRANK), F32),
        jax.ShapeDtypeStruct((B, T, C_ROPE), F32),
        jax.ShapeDtypeStruct((B, B_KW - 1, B_WIDTH), F32),
    ]
    out_specs = [
        tile(A_WIDTH), tile(B_WIDTH),
        pl.BlockSpec((1, C_HEADS, HEAD_PAD, tm), lambda b, t: (b, 0, 0, t)),
        tile(C_KVRANK), tile(C_ROPE),
        pl.BlockSpec((1, B_KW - 1, B_WIDTH), lambda b, t: (b, 0, 0)),
    ]
    if prompt:
        out_shape += [jax.ShapeDtypeStruct((B, C_HEADS, nt, tm, HEAD_PAD), BF16),
                      jax.ShapeDtypeStruct((B, C_HEADS, nt, V_ROWS, tm), BF16)]
        out_specs += [pl.BlockSpec((1, C_HEADS, 1, tm, HEAD_PAD), lambda b, t: (b, 0, t, 0, 0)),
                      pl.BlockSpec((1, C_HEADS, 1, V_ROWS, tm), lambda b, t: (b, 0, t, 0, 0))]
    else:
        out_shape += [jax.ShapeDtypeStruct((B, T, A_WIDTH), F32)]
        out_specs += [tile(A_WIDTH)]
    return pl.pallas_call(
        kern, grid=(B, nt), in_specs=in_specs, out_specs=out_specs, out_shape=out_shape,
        scratch_shapes=[pltpu.VMEM((tm + HIST_PAD + SUBLANES, B_WIDTH), F32)],
        compiler_params=_cparams(2), name=f"inproj_{tm}",
    )(x, hist, cosq, sinq, cosk, sink, *consts)


def _kvgen_kernel(lat_ref, krp_ref, wuk_ref, kng_ref, wuvT_ref, k_ref, vT_ref):
    _emit_kv(lat_ref[0], krp_ref[0], wuk_ref, kng_ref, wuvT_ref, k_ref, vT_ref)


def _kvgen(lat, krp, P, *, tk):
    B, Tk, _ = lat.shape
    nkt = Tk // tk
    consts, const_specs = _const_operands([P['w_uk'], P['kn_g'], P['w_uvT']])
    return pl.pallas_call(
        _kvgen_kernel, grid=(B, nkt),
        in_specs=[pl.BlockSpec((1, tk, C_KVRANK), lambda b, t: (b, t, 0)),
                  pl.BlockSpec((1, tk, HEAD_PAD), lambda b, t: (b, t, 0))]
                 + const_specs,
        out_specs=(pl.BlockSpec((1, C_HEADS, 1, tk, HEAD_PAD), lambda b, t: (b, 0, t, 0, 0)),
                   pl.BlockSpec((1, C_HEADS, 1, V_ROWS, tk), lambda b, t: (b, 0, t, 0, 0))),
        out_shape=(jax.ShapeDtypeStruct((B, C_HEADS, nkt, tk, HEAD_PAD), BF16),
                   jax.ShapeDtypeStruct((B, C_HEADS, nkt, V_ROWS, tk), BF16)),
        compiler_params=_cparams(2), name=f"kvgen_{tk}",
    )(lat, krp, *consts)


def _attn_kernel(qT_ref, k_ref, vT_ref, o_ref, *bufs, tq, tk, q_pos0, hg):
    qi = pl.program_id(2)

    def scores(hh, kt, masked):
        s = jnp.dot(k_ref[0, hh, kt], qT_ref[0, hh], preferred_element_type=F32)
        if masked:
            kpos = kt * tk + lax.broadcasted_iota(jnp.int32, (tk, 1), 0)
            qpos = q_pos0 + qi * tq + lax.broadcasted_iota(jnp.int32, (1, tq), 1)
            vis = lax.shift_right_logical(kpos, CHUNK_SHIFT) <= lax.shift_right_logical(qpos, CHUNK_SHIFT)
            s = jnp.where(vis, s, -1e30)
        bufs[hh][...] = s
        return jnp.max(s, axis=0, keepdims=True)

    def update(hh, kt, cmax, state):
        m, acc = state
        m_new = jnp.maximum(m, cmax)
        alpha = jnp.exp2(m - m_new)
        p = jnp.exp2(bufs[hh][...] - m_new)
        pv = jnp.dot(vT_ref[0, hh, kt], p.astype(BF16), preferred_element_type=F32)
        return m_new, alpha * acc + pv

    state0 = (jnp.full((1, tq), -1e30, F32), jnp.zeros((V_ROWS, tq), F32))
    last = hg - 1

    def sweep(kt, masked, c_pend, kt_pend, states):
        states = list(states)
        for hh in range(hg):
            c_new = scores(hh, kt, masked)
            if hh == 0:
                if c_pend is not None:
                    states[last] = update(last, kt_pend, c_pend, states[last])
            else:
                states[hh - 1] = update(hh - 1, kt, c_prev, states[hh - 1])
            c_prev = c_new
        return c_prev, tuple(states)

    c_pend, states = sweep(qi, True, None, None, (state0,) * hg)

    def body(kt, carry):
        c_pend, kt_pend, states = carry
        c_pend, states = sweep(kt, False, c_pend, kt_pend, states)
        return c_pend, kt, states

    c_pend, kt_pend, states = lax.fori_loop(0, qi, body, (c_pend, qi, states))
    states = states[:last] + (update(last, kt_pend, c_pend, states[last]),)
    for hh, (_, acc) in enumerate(states):
        inv_l = 1.0 / acc[C_VDIM:C_VDIM + 1]
        o_ref[0, hh * C_VDIM:(hh + 1) * C_VDIM, :] = (acc[0:C_VDIM] * inv_l).astype(o_ref.dtype)


def _attn(qT, k, vT, *, tq, q_pos0):
    B, H, _, Tq = qT.shape
    _, _, nkt, tk, _ = k.shape
    nq = Tq // tq
    hg = ATTN_HEAD_GROUP
    kern = functools.partial(_attn_kernel, tq=tq, tk=tk, q_pos0=q_pos0, hg=hg)
    return pl.pallas_call(
        kern, grid=(B, H // hg, nq),
        in_specs=[pl.BlockSpec((1, hg, HEAD_PAD, tq), lambda b, h, q: (b, h, 0, q)),
                  pl.BlockSpec((1, hg, nkt, tk, HEAD_PAD), lambda b, h, q: (b, h, 0, 0, 0),
                               pipeline_mode=pl.Buffered(1)),
                  pl.BlockSpec((1, hg, nkt, V_ROWS, tk), lambda b, h, q: (b, h, 0, 0, 0),
                               pipeline_mode=pl.Buffered(1))],
        out_specs=pl.BlockSpec((1, hg * C_VDIM, tq), lambda b, h, q: (b, h, q)),
        out_shape=jax.ShapeDtypeStruct((B, H * C_VDIM, Tq), BF16),
        scratch_shapes=[pltpu.VMEM((tk, tq), F32) for _ in range(hg)],
        compiler_params=_cparams(3), name=f"attn_{tq}",
    )(qT, k, vT)


def _post_kernel(x_ref, ya_ref, yb_ref, ycT_ref, mk_ref, mv_ref,
                 wout_ref, gmem_ref, wmq_ref, mqg_ref, wmo_ref, gffn_ref, w1_ref, w2_ref,
                 o_ref, *, ff_chunk):
    nb, tm, D = x_ref.shape
    ab = A_WIDTH + B_WIDTH
    x1 = []
    for b in range(nb):
        mix_ab = jnp.concatenate([ya_ref[b], yb_ref[b]], axis=1)
        xb = x_ref[b] + jnp.dot(mix_ab, wout_ref[0:ab, :], preferred_element_type=F32)
        x1.append(xb + lax.dot_general(ycT_ref[b], wout_ref[ab:, :], (((0,), (0,)), ((), ())),
                                       preferred_element_type=F32))
    x1 = x1[0] if nb == 1 else jnp.concatenate(x1, axis=0)
    qm = jnp.dot(_rms_lane(x1, gmem_ref[...]).astype(BF16), wmq_ref[...], preferred_element_type=F32)
    om = []
    for b in range(nb):
        mk = mk_ref[b].astype(BF16)
        mv = mv_ref[b].astype(BF16)
        outs = []
        for hh in range(MEM_HEADS):
            sl = slice(hh * MEM_HDIM, (hh + 1) * MEM_HDIM)
            qh = _rms_lane(qm[b * tm:(b + 1) * tm, sl], mqg_ref[...]).astype(BF16)
            s = lax.dot_general(qh, mk[:, sl], (((1,), (1,)), ((), ())),
                                preferred_element_type=F32) * (1.0 / math.sqrt(MEM_HDIM))
            s = s - jnp.max(s, axis=-1, keepdims=True)
            p = jnp.exp(s)
            p = p * (1.0 / jnp.sum(p, axis=-1, keepdims=True))
            outs.append(jnp.dot(p.astype(BF16), mv[:, sl], preferred_element_type=F32))
        om.append(jnp.concatenate(outs, axis=1).astype(BF16))
    om = om[0] if nb == 1 else jnp.concatenate(om, axis=0)
    x2 = x1 + jnp.dot(om, wmo_ref[...], preferred_element_type=F32)
    xn = _rms_lane(x2, gffn_ref[...]).astype(BF16)
    acc = x2
    d_ff = w1_ref.shape[1]
    for c in range(d_ff // ff_chunk):
        hc = jnp.dot(xn, w1_ref[:, c * ff_chunk:(c + 1) * ff_chunk], preferred_element_type=F32)
        hc = jnp.square(jnp.maximum(hc, 0.0)).astype(BF16)
        acc = acc + jnp.dot(hc, w2_ref[c * ff_chunk:(c + 1) * ff_chunk, :], preferred_element_type=F32)
    for b in range(nb):
        o_ref[b] = acc[b * tm:(b + 1) * tm]


def _post(x, ya, yb, ycT, mk, mv, P, *, tm):
    B, T, D = x.shape
    nt = T // tm
    M = mk.shape[1]
    nb = B if nt == 1 else 1
    consts, const_specs = _const_operands(
        [P['w_out'], P['g_mem'], P['w_mq'], P['mq_g'], P['w_mo'], P['g_ffn'], P['w_ff1'], P['w_ff2']])
    tile = lambda w: pl.BlockSpec((nb, tm, w), lambda b, t: (b, t, 0))
    return pl.pallas_call(
        functools.partial(_post_kernel, ff_chunk=1024), grid=(B // nb, nt),
        in_specs=[tile(D), tile(A_WIDTH), tile(B_WIDTH),
                  pl.BlockSpec((nb, C_HEADS * C_VDIM, tm), lambda b, t: (b, 0, t)),
                  pl.BlockSpec((nb, M, MEM_HEADS * MEM_HDIM), lambda b, t: (b, 0, 0)),
                  pl.BlockSpec((nb, M, MEM_HEADS * MEM_HDIM), lambda b, t: (b, 0, 0))]
                 + const_specs,
        out_specs=tile(D),
        out_shape=jax.ShapeDtypeStruct((B, T, D), F32),
        compiler_params=_cparams(2), name=f"post_{tm}",
    )(x, ya, yb, ycT, mk, mv, *consts)


def _memkv_kernel(mem_ref, g_ref, wk_ref, wv_ref, kg_ref, k_ref, v_ref):
    mn = _rms_lane(mem_ref[0], g_ref[...]).astype(BF16)
    k = jnp.dot(mn, wk_ref[...], preferred_element_type=F32)
    v_ref[0] = jnp.dot(mn, wv_ref[...], preferred_element_type=F32)
    for hh in range(MEM_HEADS):
        sl = slice(hh * MEM_HDIM, (hh + 1) * MEM_HDIM)
        k_ref[0, :, sl] = _rms_lane(k[:, sl], kg_ref[...])


def _memkv(mem, P):
    B, M, D = mem.shape
    W = MEM_HEADS * MEM_HDIM
    consts, const_specs = _const_operands([P['g_memkv'], P['w_mk'], P['w_mv'], P['mk_g']])
    return pl.pallas_call(
        _memkv_kernel, grid=(B,),
        in_specs=[pl.BlockSpec((1, M, D), lambda b: (b, 0, 0))] + const_specs,
        out_specs=(pl.BlockSpec((1, M, W), lambda b: (b, 0, 0)), pl.BlockSpec((1, M, W), lambda b: (b, 0, 0))),
        out_shape=(jax.ShapeDtypeStruct((B, M, W), F32), jax.ShapeDtypeStruct((B, M, W), F32)),
        compiler_params=_cparams(1), name="memkv",
    )(mem, *consts)


def _row(v):
    return v.reshape(1, -1).astype(F32)


STACKED_WEIGHTS = ('w_out', 'w_mq', 'w_mo', 'w_ff1', 'w_ff2', 'w_mk', 'w_mv')


def _prep_layer(l, W, Wb, L):
    P = {}
    P['g_mix'] = _row(W['norm_mix_g'][l])
    P['w_in'] = jnp.pad(W['w_in'][l], ((0, 0), (0, IN_PAD - W['w_in'].shape[2]))).astype(BF16)
    P['a_g'] = _row(W['a_norm_g'][l])
    P['a_ws'] = W['a_ws'][l][:, :L, :L]
    P['a_bias'] = jnp.repeat(W['a_bs'][l][:, :L].T, A_GDIM, axis=1)
    P['dw_w'] = W['b_dw_w'][l]
    P['dw_b'] = _row(W['b_dw_b'][l])
    P['ln_g'] = _row(W['b_ln_g'][l])
    P['ln_b'] = _row(W['b_ln_b'][l])
    P['qa_g'] = _row(W['c_qa_g'][l])
    wq = W['c_w_uq'][l].reshape(C_QRANK, C_HEADS, C_NOPE + C_ROPE)
    wq = jnp.pad(wq, ((0, 0), (0, 0), (0, HEAD_PAD - C_NOPE - C_ROPE)))
    P['w_uqT'] = wq.reshape(C_QRANK, C_HEADS * HEAD_PAD).T.astype(BF16)
    P['qn_g'] = W['c_qn_g'][l].reshape(C_NOPE, 1)
    P['qr_g'] = W['c_qr_g'][l].reshape(C_ROPE, 1)
    P['kva_g'] = _row(W['c_kva_g'][l])
    P['kr_g'] = jnp.pad(_row(W['c_kr_g'][l]), ((0, 0), (0, 128 - C_ROPE)))
    wkv = W['c_w_ukv'][l].reshape(C_KVRANK, C_HEADS, C_NOPE + C_VDIM)
    P['w_uk'] = jnp.pad(wkv[:, :, :C_NOPE], ((0, 0), (0, 0), (0, HEAD_PAD - C_NOPE))
                        ).reshape(C_KVRANK, C_HEADS * HEAD_PAD).astype(BF16)
    P['w_uvT'] = wkv[:, :, C_NOPE:].reshape(C_KVRANK, C_HEADS * C_VDIM).T.astype(BF16)
    P['kn_g'] = jnp.pad(_row(W['c_kn_g'][l]), ((0, 0), (0, HEAD_PAD - C_NOPE)))
    P['g_mem'] = _row(W['norm_mem_g'][l])
    P['mq_g'] = _row(W['m_q_g'][l])
    P['g_ffn'] = _row(W['norm_ffn_g'][l])
    P['g_memkv'] = _row(W['mem_norm_g'][l])
    P['mk_g'] = _row(W['m_k_g'][l])
    for name in STACKED_WEIGHTS:
        P[name] = (Wb[name], l)
    return P


def _rope_tables(start, t):
    half = C_ROPE // 2
    inv = ROPE_THETA ** (-jnp.arange(half, dtype=F32) / half)
    ang = (start + jnp.arange(t)).astype(F32)[:, None] * inv[None, :]
    cos, sin = jnp.cos(ang), jnp.sin(ang)
    zeros = jnp.zeros((t, 128 - C_ROPE), F32)
    cosk = jnp.concatenate([cos, cos, zeros], axis=1)
    sink = jnp.concatenate([-sin, sin, zeros], axis=1)
    return cos.T, sin.T, cosk, sink


def _pad_kr(kr):
    return jnp.pad(kr, ((0, 0), (0, 0), (C_NOPE, HEAD_PAD - C_NOPE - C_ROPE)))


def _layer(x, hist, tabs, lat_past, kr_past, mk, mv, P, *, tm, L, tq, start):
    if lat_past is None:
        ya, yb, qT, lat, kr, cstate, k, vT = _inproj(x, hist, tabs, P, tm=tm, L=L, prompt=True)
        gv = None
    else:
        ya, yb, qT, lat, kr, cstate, gv = _inproj(x, hist, tabs, P, tm=tm, L=L, prompt=False)
        lat_all = jnp.concatenate([lat_past, lat], axis=1)
        kr_all = jnp.concatenate([kr_past, kr], axis=1)
        k, vT = _kvgen(lat_all, _pad_kr(kr_all), P, tk=lat_all.shape[1])
    ycT = _attn(qT, k, vT, tq=tq, q_pos0=start)
    y = _post(x, ya, yb, ycT, mk, mv, P, tm=tm)
    return y, gv, cstate, lat, kr


def kernel(x_prompt, x_sample, mem_prompt, cache_mla_latent, cache_mla_krope, cache_conv, cache_mem_k, cache_mem_v, norm_mix_g, w_in, a_norm_g, a_ws, a_bs, b_dw_w, b_dw_b, b_ln_g, b_ln_b, c_qa_g, c_w_uq, c_kva_g, c_w_ukv, c_qn_g, c_qr_g, c_kn_g, c_kr_g, w_out, norm_mem_g, mem_norm_g, w_mq, w_mk, w_mv, w_mo, m_q_g, m_k_g, norm_ffn_g, w_ff1, w_ff2):
    W = dict(norm_mix_g=norm_mix_g, w_in=w_in, a_norm_g=a_norm_g, a_ws=a_ws, a_bs=a_bs,
             b_dw_w=b_dw_w, b_dw_b=b_dw_b, b_ln_g=b_ln_g, b_ln_b=b_ln_b, c_qa_g=c_qa_g,
             c_w_uq=c_w_uq, c_kva_g=c_kva_g, c_w_ukv=c_w_ukv, c_qn_g=c_qn_g, c_qr_g=c_qr_g,
             c_kn_g=c_kn_g, c_kr_g=c_kr_g, w_out=w_out, norm_mem_g=norm_mem_g,
             mem_norm_g=mem_norm_g, w_mq=w_mq, w_mk=w_mk, w_mv=w_mv, w_mo=w_mo, m_q_g=m_q_g,
             m_k_g=m_k_g, norm_ffn_g=norm_ffn_g, w_ff1=w_ff1, w_ff2=w_ff2)
    depth = w_in.shape[0]
    Bp, Tp, _ = x_prompt.shape
    Bs, Ts, _ = x_sample.shape
    past = cache_mla_latent.shape[2]
    M = mem_prompt.shape[1]
    MW = MEM_HEADS * MEM_HDIM

    tm_p = min(512, Tp)
    L_p = min(Tp, A_CHUNK)
    L_s = min(Ts, A_CHUNK)
    tabs_p = _rope_tables(0, Tp)
    tabs_s = _rope_tables(past, Ts)
    hist0 = jnp.zeros((Bp, B_KW - 1, B_WIDTH), F32)

    hp, hs = x_prompt, x_sample
    lat_p, kr_p, conv_p, mk_p, mv_p = [], [], [], [], []
    lat_s, kr_s, conv_s, gv_s = [], [], [], []
    Wb = {name: W[name].astype(BF16) for name in STACKED_WEIGHTS}
    for l in range(depth):
        Pp = _prep_layer(l, W, Wb, L_p)
        Ps = Pp if L_s == L_p else dict(Pp, a_ws=W['a_ws'][l][:, :L_s, :L_s],
                                        a_bias=jnp.repeat(W['a_bs'][l][:, :L_s].T, A_GDIM, axis=1))
        mk, mv = _memkv(mem_prompt, Pp)
        hp, _, cst, lat, kr = _layer(hp, hist0, tabs_p, None, None, mk, mv, Pp,
                                     tm=tm_p, L=L_p, tq=tm_p, start=0)
        lat_p.append(lat); kr_p.append(kr); conv_p.append(cst)
        mk_p.append(mk.reshape(Bp, M, MEM_HEADS, MEM_HDIM)); mv_p.append(mv.reshape(Bp, M, MEM_HEADS, MEM_HDIM))
        hs, gv, cst, lat, kr = _layer(hs, cache_conv[l], tabs_s, cache_mla_latent[l], cache_mla_krope[l],
                                      cache_mem_k[l].reshape(Bs, M, MW), cache_mem_v[l].reshape(Bs, M, MW), Ps,
                                      tm=Ts, L=L_s, tq=Ts, start=past)
        lat_s.append(lat); kr_s.append(kr); conv_s.append(cst); gv_s.append(gv)
    return (hp, hs,
            jnp.stack(lat_p), jnp.stack(kr_p), jnp.stack(conv_p), jnp.stack(mk_p), jnp.stack(mv_p),
            jnp.stack(lat_s), jnp.stack(kr_s), jnp.stack(conv_s), jnp.stack(gv_s))
```

```python
import functools
import math

import jax
import jax.numpy as jnp
from jax import lax
from jax.experimental import pallas as pl
from jax.experimental.pallas import tpu as pltpu

F32 = jnp.float32
BF16 = jnp.bfloat16

EPS = 1e-6
CHUNK_SHIFT = 6
A_WIDTH = 256
A_GROUPS = 4
A_GDIM = 64
A_CHUNK = 128
B_WIDTH = 256
B_KW = 31
C_HEADS = 8
C_NOPE = 64
C_ROPE = 32
C_VDIM = 64
C_QRANK = 256
C_KVRANK = 128
ROPE_THETA = 10000.0
MEM_HEADS = 4
MEM_HDIM = 128
V_ROWS = C_VDIM + 16
HEAD_PAD = 128
IN_PAD = 1536
SUBLANES = 8
HIST_PAD = 32
QK_SCALE = math.log2(math.e) / math.sqrt(C_NOPE + C_ROPE)

ATTN_HEAD_GROUP = 8
VMEM_LIMIT = 56 * 1024 * 1024


def _cparams(n_axes):
    return pltpu.CompilerParams(dimension_semantics=("arbitrary",) * n_axes,
                                vmem_limit_bytes=VMEM_LIMIT)


def _rms_lane(x, g):
    return x * lax.rsqrt(jnp.mean(x * x, axis=-1, keepdims=True) + EPS) * g


def _gelu(x):
    return 0.5 * x * (1.0 + lax.erf(x * (1.0 / math.sqrt(2.0))))


def _const_operands(items):
    arrays, specs = [], []
    for it in items:
        if isinstance(it, tuple):
            arr, layer = it
            nd = arr.ndim - 1
            spec = pl.BlockSpec((None,) + arr.shape[1:], lambda *_, layer=layer, nd=nd: (layer,) + (0,) * nd,
                                pipeline_mode=pl.Buffered(1))
        else:
            arr = it
            spec = pl.BlockSpec(arr.shape, lambda *_, nd=arr.ndim: (0,) * nd, pipeline_mode=pl.Buffered(1))
        arrays.append(arr)
        specs.append(spec)
    return arrays, specs


N_INPROJ_IN = 21


def _emit_kv(lat, krp, wuk_ref, kng_ref, wuvT_ref, k_ref, vT_ref):
    latb = lat.astype(BF16)
    kf = jnp.dot(latb, wuk_ref[...], preferred_element_type=F32)
    g = kng_ref[...]
    vT = lax.dot_general(wuvT_ref[...], latb, (((1,), (1,)), ((), ())),
                         preferred_element_type=F32)
    ones = jnp.ones((V_ROWS - C_VDIM, vT.shape[1]), vT_ref.dtype)
    for hh in range(C_HEADS):
        kh = kf[:, hh * HEAD_PAD:(hh + 1) * HEAD_PAD]
        ms = jnp.sum(kh * kh, axis=-1, keepdims=True) * (1.0 / C_NOPE)
        k_ref[0, hh, 0] = (kh * lax.rsqrt(ms + EPS) * g + krp).astype(k_ref.dtype)
        vT_ref[0, hh, 0, 0:C_VDIM, :] = vT[hh * C_VDIM:(hh + 1) * C_VDIM].astype(vT_ref.dtype)
        vT_ref[0, hh, 0, C_VDIM:V_ROWS, :] = ones


def _inproj_kernel(*refs, tm, L, prompt):
    (x_ref, hist_ref, cosq_ref, sinq_ref, cosk_ref, sink_ref,
     gmix_ref, win_ref, ag_ref, aws_ref, abias_ref,
     dww_ref, dwb_ref, lng_ref, lnb_ref,
     qag_ref, wuqT_ref, qng_ref, qrg_ref, kvag_ref, krg_ref) = refs[:N_INPROJ_IN]
    n_in = N_INPROJ_IN + (3 if prompt else 0)
    ya_ref, yb_ref, qT_ref, lat_ref, kr_ref, cs_ref = refs[n_in:n_in + 6]
    zbuf = refs[-1]
    t = pl.program_id(1)
    x = x_ref[0]
    xn = _rms_lane(x, gmix_ref[...])
    h = jnp.dot(xn.astype(BF16), win_ref[...], preferred_element_type=F32)
    a_u = h[:, 0:256]
    a_v = h[:, 256:512]
    b_a = h[:, 512:768]
    b_g = h[:, 768:1024]
    c_q = h[:, 1024:1280]
    c_kv = h[:, 1280:1408]
    c_kr = h[:, 1408:1536]

    u = _gelu(a_u)
    v = _rms_lane(_gelu(a_v), ag_ref[...])
    if not prompt:
        refs[n_in + 6][0] = v
    vb = v.astype(BF16)
    row = lax.broadcasted_iota(jnp.int32, (L, L), 0)
    col = lax.broadcasted_iota(jnp.int32, (L, L), 1)
    tril = col <= row
    wcat = jnp.concatenate(
        [jnp.where(tril, aws_ref[g], 0.0).astype(BF16) for g in range(A_GROUPS)], axis=1)
    lane = lax.broadcasted_iota(jnp.int32, (L, A_WIDTH), 1)
    gmasks = [(lane >= g * A_GDIM) & (lane < (g + 1) * A_GDIM) for g in range(A_GROUPS)]
    zero_b = jnp.zeros((L, A_WIDTH), BF16)
    bias = abias_ref[...]
    for c in range(tm // L):
        vc = vb[c * L:(c + 1) * L]
        vstack = jnp.concatenate([jnp.where(gmasks[g], vc, zero_b) for g in range(A_GROUPS)], axis=0)
        mixed = jnp.dot(wcat, vstack, preferred_element_type=F32) + bias
        ya_ref[0, c * L:(c + 1) * L, :] = (u[c * L:(c + 1) * L] * mixed).astype(ya_ref.dtype)

    z = b_a * jax.nn.sigmoid(b_g)

    @pl.when(t == 0)
    def _():
        zbuf[0:8, :] = jnp.zeros((8, B_WIDTH), F32)
        zbuf[HIST_PAD - (B_KW - 1):HIST_PAD, :] = hist_ref[0]

    @pl.when(t > 0)
    def _():
        zbuf[0:HIST_PAD, :] = zbuf[tm:tm + HIST_PAD, :]

    zbuf[HIST_PAD:HIST_PAD + tm, :] = z
    zbuf[HIST_PAD + tm:HIST_PAD + tm + SUBLANES, :] = jnp.zeros((SUBLANES, B_WIDTH), F32)
    off = HIST_PAD - (B_KW - 1)
    n_win = tm + SUBLANES
    acc = None
    for a in range(SUBLANES):
        va = None
        for blk in range(HIST_PAD // SUBLANES + 1):
            j = SUBLANES * blk + a - off
            if 0 <= j < B_KW:
                term = dww_ref[j:j + 1, :] * zbuf[SUBLANES * blk:SUBLANES * blk + n_win, :]
                va = term if va is None else va + term
        if a:
            va = pltpu.roll(va, n_win - a, 0)
        acc = va[0:tm] if acc is None else acc + va[0:tm]
    acc = acc + dwb_ref[...]
    cs_ref[0] = zbuf[tm + off:tm + HIST_PAD, :]
    mu = jnp.mean(acc, axis=-1, keepdims=True)
    xc = acc - mu
    yn = xc * lax.rsqrt(jnp.mean(xc * xc, axis=-1, keepdims=True) + EPS) * lng_ref[...] + lnb_ref[...]
    yb_ref[0] = (yn * jax.nn.sigmoid(yn)).astype(yb_ref.dtype)

    qa = _rms_lane(c_q, qag_ref[...]).astype(BF16)
    qT = lax.dot_general(wuqT_ref[...], qa, (((1,), (1,)), ((), ())),
                         preferred_element_type=F32)
    cq = cosq_ref[...]
    sq = sinq_ref[...]
    gqn = qng_ref[...] * QK_SCALE
    gqr = qrg_ref[...]
    half = C_ROPE // 2
    zpad = jnp.zeros((HEAD_PAD - C_NOPE - C_ROPE, tm), F32)
    for hh in range(C_HEADS):
        blk = qT[hh * HEAD_PAD:(hh + 1) * HEAD_PAD]
        qn = blk[0:C_NOPE]
        qr = blk[C_NOPE:C_NOPE + C_ROPE]
        qn = qn * lax.rsqrt(jnp.mean(qn * qn, axis=0, keepdims=True) + EPS) * gqn
        qr = qr * lax.rsqrt(jnp.mean(qr * qr, axis=0, keepdims=True) + EPS) * gqr
        x1 = qr[0:half]
        x2 = qr[half:C_ROPE]
        r1 = (x1 * cq - x2 * sq) * QK_SCALE
        r2 = (x1 * sq + x2 * cq) * QK_SCALE
        qT_ref[0, hh] = jnp.concatenate([qn, r1, r2, zpad], axis=0).astype(qT_ref.dtype)

    lat = _rms_lane(c_kv, kvag_ref[...])
    lat_ref[0] = lat
    krn = c_kr * lax.rsqrt(jnp.sum(c_kr * c_kr, axis=-1, keepdims=True) * (1.0 / C_ROPE) + EPS) * krg_ref[...]
    lane_k = lax.broadcasted_iota(jnp.int32, krn.shape, 1)
    partner = jnp.where(lane_k < half, pltpu.roll(krn, 128 - half, 1), pltpu.roll(krn, half, 1))
    kro = krn * cosk_ref[...] + partner * sink_ref[...]
    kr_ref[0] = kro[:, 0:C_ROPE]
    if prompt:
        wuk_ref, kng_ref, wuvT_ref = refs[N_INPROJ_IN:n_in]
        k_ref, vT_ref = refs[n_in + 6:n_in + 8]
        _emit_kv(lat, pltpu.roll(kro, C_NOPE, 1), wuk_ref, kng_ref, wuvT_ref, k_ref, vT_ref)


def _inproj(x, hist, tabs, P, *, tm, L, prompt):
    B, T, D = x.shape
    nt = T // tm
    cosq, sinq, cosk, sink = tabs
    kern = functools.partial(_inproj_kernel, tm=tm, L=L, prompt=prompt)
    tile = lambda w: pl.BlockSpec((1, tm, w), lambda b, t: (b, t, 0))
    in_specs = [
        tile(D),
        pl.BlockSpec((1, B_KW - 1, B_WIDTH), lambda b, t: (b, 0, 0)),
        pl.BlockSpec((C_ROPE // 2, tm), lambda b, t: (0, t)),
        pl.BlockSpec((C_ROPE // 2, tm), lambda b, t: (0, t)),
        pl.BlockSpec((tm, 128), lambda b, t: (t, 0)),
        pl.BlockSpec((tm, 128), lambda b, t: (t, 0)),
    ]
    consts = [P['g_mix'], P['w_in'], P['a_g'], P['a_ws'], P['a_bias'],
              P['dw_w'], P['dw_b'], P['ln_g'], P['ln_b'],
              P['qa_g'], P['w_uqT'], P['qn_g'], P['qr_g'], P['kva_g'], P['kr_g']]
    if prompt:
        consts += [P['w_uk'], P['kn_g'], P['w_uvT']]
    consts, const_specs = _const_operands(consts)
    in_specs += const_specs
    out_shape = [
        jax.ShapeDtypeStruct((B, T, A_WIDTH), BF16),
        jax.ShapeDtypeStruct((B, T, B_WIDTH), BF16),
        jax.ShapeDtypeStruct((B, C_HEADS, HEAD_PAD, T), BF16),
        jax.ShapeDtypeStruct((B, T, C_KVRANK), F32),
        jax.ShapeDtypeStruct((B, T, C_ROPE), F32),
        jax.ShapeDtypeStruct((B, B_KW - 1, B_WIDTH), F32),
    ]
    out_specs = [
        tile(A_WIDTH), tile(B_WIDTH),
        pl.BlockSpec((1, C_HEADS, HEAD_PAD, tm), lambda b, t: (b, 0, 0, t)),
        tile(C_KVRANK), tile(C_ROPE),
        pl.BlockSpec((1, B_KW - 1, B_WIDTH), lambda b, t: (b, 0, 0)),
    ]
    if prompt:
        out_shape += [jax.ShapeDtypeStruct((B, C_HEADS, nt, tm, HEAD_PAD), BF16),
                      jax.ShapeDtypeStruct((B, C_HEADS, nt, V_ROWS, tm), BF16)]
        out_specs += [pl.BlockSpec((1, C_HEADS, 1, tm, HEAD_PAD), lambda b, t: (b, 0, t, 0, 0)),
                      pl.BlockSpec((1, C_HEADS, 1, V_ROWS, tm), lambda b, t: (b, 0, t, 0, 0))]
    else:
        out_shape += [jax.ShapeDtypeStruct((B, T, A_WIDTH), F32)]
        out_specs += [tile(A_WIDTH)]
    return pl.pallas_call(
        kern, grid=(B, nt), in_specs=in_specs, out_specs=out_specs, out_shape=out_shape,
        scratch_shapes=[pltpu.VMEM((tm + HIST_PAD + SUBLANES, B_WIDTH), F32)],
        compiler_params=_cparams(2), name=f"inproj_{tm}",
    )(x, hist, cosq, sinq, cosk, sink, *consts)


def _kvgen_kernel(lat_ref, krp_ref, wuk_ref, kng_ref, wuvT_ref, k_ref, vT_ref):
    _emit_kv(lat_ref[0], krp_ref[0], wuk_ref, kng_ref, wuvT_ref, k_ref, vT_ref)


def _kvgen(lat, krp, P, *, tk):
    B, Tk, _ = lat.shape
    nkt = Tk // tk
    consts, const_specs = _const_operands([P['w_uk'], P['kn_g'], P['w_uvT']])
    return pl.pallas_call(
        _kvgen_kernel, grid=(B, nkt),
        in_specs=[pl.BlockSpec((1, tk, C_KVRANK), lambda b, t: (b, t, 0)),
                  pl.BlockSpec((1, tk, HEAD_PAD), lambda b, t: (b, t, 0))]
                 + const_specs,
        out_specs=(pl.BlockSpec((1, C_HEADS, 1, tk, HEAD_PAD), lambda b, t: (b, 0, t, 0, 0)),
                   pl.BlockSpec((1, C_HEADS, 1, V_ROWS, tk), lambda b, t: (b, 0, t, 0, 0))),
        out_shape=(jax.ShapeDtypeStruct((B, C_HEADS, nkt, tk, HEAD_PAD), BF16),
                   jax.ShapeDtypeStruct((B, C_HEADS, nkt, V_ROWS, tk), BF16)),
        compiler_params=_cparams(2), name=f"kvgen_{tk}",
    )(lat, krp, *consts)


def _attn_kernel(qT_ref, k_ref, vT_ref, o_ref, *bufs, tq, tk, q_pos0, hg):
    qi = pl.program_id(2)

    def scores(hh, kt, vis):
        s = jnp.dot(k_ref[0, hh, kt], qT_ref[0, hh], preferred_element_type=F32)
        if vis is not None:
            s = jnp.where(vis, s, -1e30)
        bufs[hh][...] = s
        return jnp.max(s, axis=0, keepdims=True)

    def update(hh, kt, cmax, state):
        m, acc = state
        m_new = jnp.maximum(m, cmax)
        alpha = jnp.exp2(m - m_new)
        p = jnp.exp2(bufs[hh][...] - m_new)
        pv = jnp.dot(vT_ref[0, hh, kt], p.astype(BF16), preferred_element_type=F32)
        return m_new, alpha * acc + pv

    state0 = (jnp.full((1, tq), -1e30, F32), jnp.zeros((V_ROWS, tq), F32))
    last = hg - 1

    def sweep(kt, vis, c_pend, kt_pend, states):
        states = list(states)
        for hh in range(hg):
            c_new = scores(hh, kt, vis)
            if hh == 0:
                if c_pend is not None:
                    states[last] = update(last, kt_pend, c_pend, states[last])
            else:
                states[hh - 1] = update(hh - 1, kt, c_prev, states[hh - 1])
            c_prev = c_new
        return c_prev, kt, tuple(states)

    kpos = qi * tk + lax.broadcasted_iota(jnp.int32, (tk, 1), 0)
    qpos = q_pos0 + qi * tq + lax.broadcasted_iota(jnp.int32, (1, tq), 1)
    vis_diag = lax.shift_right_logical(kpos, CHUNK_SHIFT) <= lax.shift_right_logical(qpos, CHUNK_SHIFT)
    carry = sweep(qi, vis_diag, None, None, (state0,) * hg)

    def pair(i, carry):
        carry = sweep(2 * i, None, *carry)
        return sweep(2 * i + 1, None, *carry)

    carry = lax.fori_loop(0, lax.shift_right_logical(qi, 1), pair, carry)
    carry = lax.cond(lax.rem(qi, 2) == 1, lambda c: sweep(qi - 1, None, *c), lambda c: c, carry)
    c_pend, kt_pend, states = carry
    states = states[:last] + (update(last, kt_pend, c_pend, states[last]),)
    for hh, (_, acc) in enumerate(states):
        inv_l = 1.0 / acc[C_VDIM:C_VDIM + 1]
        o_ref[0, hh * C_VDIM:(hh + 1) * C_VDIM, :] = (acc[0:C_VDIM] * inv_l).astype(o_ref.dtype)


def _attn(qT, k, vT, *, tq, q_pos0):
    B, H, _, Tq = qT.shape
    _, _, nkt, tk, _ = k.shape
    nq = Tq // tq
    hg = ATTN_HEAD_GROUP
    kern = functools.partial(_attn_kernel, tq=tq, tk=tk, q_pos0=q_pos0, hg=hg)
    return pl.pallas_call(
        kern, grid=(B, H // hg, nq),
        in_specs=[pl.BlockSpec((1, hg, HEAD_PAD, tq), lambda b, h, q: (b, h, 0, q)),
                  pl.BlockSpec((1, hg, nkt, tk, HEAD_PAD), lambda b, h, q: (b, h, 0, 0, 0),
                               pipeline_mode=pl.Buffered(1)),
                  pl.BlockSpec((1, hg, nkt, V_ROWS, tk), lambda b, h, q: (b, h, 0, 0, 0),
                               pipeline_mode=pl.Buffered(1))],
        out_specs=pl.BlockSpec((1, hg * C_VDIM, tq), lambda b, h, q: (b, h, q)),
        out_shape=jax.ShapeDtypeStruct((B, H * C_VDIM, Tq), BF16),
        scratch_shapes=[pltpu.VMEM((tk, tq), F32) for _ in range(hg)],
        compiler_params=_cparams(3), name=f"attn_{tq}",
    )(qT, k, vT)


def _post_kernel(x_ref, ya_ref, yb_ref, ycT_ref, mk_ref, mv_ref,
                 wout_ref, gmem_ref, wmq_ref, mqg_ref, wmo_ref, gffn_ref, w1_ref, w2_ref,
                 o_ref, *, ff_chunk):
    nb, tm, D = x_ref.shape
    ab = A_WIDTH + B_WIDTH
    x1 = []
    for b in range(nb):
        mix_ab = jnp.concatenate([ya_ref[b], yb_ref[b]], axis=1)
        xb = x_ref[b] + jnp.dot(mix_ab, wout_ref[0:ab, :], preferred_element_type=F32)
        x1.append(xb + lax.dot_general(ycT_ref[b], wout_ref[ab:, :], (((0,), (0,)), ((), ())),
                                       preferred_element_type=F32))
    x1 = x1[0] if nb == 1 else jnp.concatenate(x1, axis=0)
    qm = jnp.dot(_rms_lane(x1, gmem_ref[...]).astype(BF16), wmq_ref[...], preferred_element_type=F32)
    om = []
    for b in range(nb):
        mk = mk_ref[b].astype(BF16)
        mv = mv_ref[b].astype(BF16)
        outs = []
        for hh in range(MEM_HEADS):
            sl = slice(hh * MEM_HDIM, (hh + 1) * MEM_HDIM)
            qh = _rms_lane(qm[b * tm:(b + 1) * tm, sl], mqg_ref[...]).astype(BF16)
            s = lax.dot_general(qh, mk[:, sl], (((1,), (1,)), ((), ())),
                                preferred_element_type=F32) * (1.0 / math.sqrt(MEM_HDIM))
            s = s - jnp.max(s, axis=-1, keepdims=True)
            p = jnp.exp(s)
            p = p * (1.0 / jnp.sum(p, axis=-1, keepdims=True))
            outs.append(jnp.dot(p.astype(BF16), mv[:, sl], preferred_element_type=F32))
        om.append(jnp.concatenate(outs, axis=1).astype(BF16))
    om = om[0] if nb == 1 else jnp.concatenate(om, axis=0)
    x2 = x1 + jnp.dot(om, wmo_ref[...], preferred_element_type=F32)
    xn = _rms_lane(x2, gffn_ref[...]).astype(BF16)
    acc = x2
    d_ff = w1_ref.shape[1]
    for c in range(d_ff // ff_chunk):
        hc = jnp.dot(xn, w1_ref[:, c * ff_chunk:(c + 1) * ff_chunk], preferred_element_type=F32)
        hc = jnp.square(jnp.maximum(hc, 0.0)).astype(BF16)
        acc = acc + jnp.dot(hc, w2_ref[c * ff_chunk:(c + 1) * ff_chunk, :], preferred_element_type=F32)
    for b in range(nb):
        o_ref[b] = acc[b * tm:(b + 1) * tm]


def _post(x, ya, yb, ycT, mk, mv, P, *, tm):
    B, T, D = x.shape
    nt = T // tm
    M = mk.shape[1]
    nb = B if nt == 1 else 1
    consts, const_specs = _const_operands(
        [P['w_out'], P['g_mem'], P['w_mq'], P['mq_g'], P['w_mo'], P['g_ffn'], P['w_ff1'], P['w_ff2']])
    tile = lambda w: pl.BlockSpec((nb, tm, w), lambda b, t: (b, t, 0))
    return pl.pallas_call(
        functools.partial(_post_kernel, ff_chunk=1024), grid=(B // nb, nt),
        in_specs=[tile(D), tile(A_WIDTH), tile(B_WIDTH),
                  pl.BlockSpec((nb, C_HEADS * C_VDIM, tm), lambda b, t: (b, 0, t)),
                  pl.BlockSpec((nb, M, MEM_HEADS * MEM_HDIM), lambda b, t: (b, 0, 0)),
                  pl.BlockSpec((nb, M, MEM_HEADS * MEM_HDIM), lambda b, t: (b, 0, 0))]
                 + const_specs,
        out_specs=tile(D),
        out_shape=jax.ShapeDtypeStruct((B, T, D), F32),
        compiler_params=_cparams(2), name=f"post_{tm}",
    )(x, ya, yb, ycT, mk, mv, *consts)


def _memkv_kernel(mem_ref, g_ref, wk_ref, wv_ref, kg_ref, k_ref, v_ref):
    mn = _rms_lane(mem_ref[0], g_ref[...]).astype(BF16)
    k = jnp.dot(mn, wk_ref[...], preferred_element_type=F32)
    v_ref[0] = jnp.dot(mn, wv_ref[...], preferred_element_type=F32)
    for hh in range(MEM_HEADS):
        sl = slice(hh * MEM_HDIM, (hh + 1) * MEM_HDIM)
        k_ref[0, :, sl] = _rms_lane(k[:, sl], kg_ref[...])


def _memkv(mem, P):
    B, M, D = mem.shape
    W = MEM_HEADS * MEM_HDIM
    consts, const_specs = _const_operands([P['g_memkv'], P['w_mk'], P['w_mv'], P['mk_g']])
    return pl.pallas_call(
        _memkv_kernel, grid=(B,),
        in_specs=[pl.BlockSpec((1, M, D), lambda b: (b, 0, 0))] + const_specs,
        out_specs=(pl.BlockSpec((1, M, W), lambda b: (b, 0, 0)), pl.BlockSpec((1, M, W), lambda b: (b, 0, 0))),
        out_shape=(jax.ShapeDtypeStruct((B, M, W), F32), jax.ShapeDtypeStruct((B, M, W), F32)),
        compiler_params=_cparams(1), name="memkv",
    )(mem, *consts)


def _row(v):
    return v.reshape(1, -1).astype(F32)


STACKED_WEIGHTS = ('w_out', 'w_mq', 'w_mo', 'w_ff1', 'w_ff2', 'w_mk', 'w_mv')


def _prep_layer(l, W, Wb, L):
    P = {}
    P['g_mix'] = _row(W['norm_mix_g'][l])
    P['w_in'] = jnp.pad(W['w_in'][l], ((0, 0), (0, IN_PAD - W['w_in'].shape[2]))).astype(BF16)
    P['a_g'] = _row(W['a_norm_g'][l])
    P['a_ws'] = W['a_ws'][l][:, :L, :L]
    P['a_bias'] = jnp.repeat(W['a_bs'][l][:, :L].T, A_GDIM, axis=1)
    P['dw_w'] = W['b_dw_w'][l]
    P['dw_b'] = _row(W['b_dw_b'][l])
    P['ln_g'] = _row(W['b_ln_g'][l])
    P['ln_b'] = _row(W['b_ln_b'][l])
    P['qa_g'] = _row(W['c_qa_g'][l])
    wq = W['c_w_uq'][l].reshape(C_QRANK, C_HEADS, C_NOPE + C_ROPE)
    wq = jnp.pad(wq, ((0, 0), (0, 0), (0, HEAD_PAD - C_NOPE - C_ROPE)))
    P['w_uqT'] = wq.reshape(C_QRANK, C_HEADS * HEAD_PAD).T.astype(BF16)
    P['qn_g'] = W['c_qn_g'][l].reshape(C_NOPE, 1)
    P['qr_g'] = W['c_qr_g'][l].reshape(C_ROPE, 1)
    P['kva_g'] = _row(W['c_kva_g'][l])
    P['kr_g'] = jnp.pad(_row(W['c_kr_g'][l]), ((0, 0), (0, 128 - C_ROPE)))
    wkv = W['c_w_ukv'][l].reshape(C_KVRANK, C_HEADS, C_NOPE + C_VDIM)
    P['w_uk'] = jnp.pad(wkv[:, :, :C_NOPE], ((0, 0), (0, 0), (0, HEAD_PAD - C_NOPE))
                        ).reshape(C_KVRANK, C_HEADS * HEAD_PAD).astype(BF16)
    P['w_uvT'] = wkv[:, :, C_NOPE:].reshape(C_KVRANK, C_HEADS * C_VDIM).T.astype(BF16)
    P['kn_g'] = jnp.pad(_row(W['c_kn_g'][l]), ((0, 0), (0, HEAD_PAD - C_NOPE)))
    P['g_mem'] = _row(W['norm_mem_g'][l])
    P['mq_g'] = _row(W['m_q_g'][l])
    P['g_ffn'] = _row(W['norm_ffn_g'][l])
    P['g_memkv'] = _row(W['mem_norm_g'][l])
    P['mk_g'] = _row(W['m_k_g'][l])
    for name in STACKED_WEIGHTS:
        P[name] = (Wb[name], l)
    return P


def _rope_tables(start, t):
    half = C_ROPE // 2
    inv = ROPE_THETA ** (-jnp.arange(half, dtype=F32) / half)
    ang = (start + jnp.arange(t)).astype(F32)[:, None] * inv[None, :]
    cos, sin = jnp.cos(ang), jnp.sin(ang)
    zeros = jnp.zeros((t, 128 - C_ROPE), F32)
    cosk = jnp.concatenate([cos, cos, zeros], axis=1)
    sink = jnp.concatenate([-sin, sin, zeros], axis=1)
    return cos.T, sin.T, cosk, sink


def _pad_kr(kr):
    return jnp.pad(kr, ((0, 0), (0, 0), (C_NOPE, HEAD_PAD - C_NOPE - C_ROPE)))


def _layer(x, hist, tabs, lat_past, kr_past, mk, mv, P, *, tm, L, tq, start):
    if lat_past is None:
        ya, yb, qT, lat, kr, cstate, k, vT = _inproj(x, hist, tabs, P, tm=tm, L=L, prompt=True)
        gv = None
    else:
        ya, yb, qT, lat, kr, cstate, gv = _inproj(x, hist, tabs, P, tm=tm, L=L, prompt=False)
        lat_all = jnp.concatenate([lat_past, lat], axis=1)
        kr_all = jnp.concatenate([kr_past, kr], axis=1)
        k, vT = _kvgen(lat_all, _pad_kr(kr_all), P, tk=lat_all.shape[1])
    ycT = _attn(qT, k, vT, tq=tq, q_pos0=start)
    y = _post(x, ya, yb, ycT, mk, mv, P, tm=tm)
    return y, gv, cstate, lat, kr


def kernel(x_prompt, x_sample, mem_prompt, cache_mla_latent, cache_mla_krope, cache_conv, cache_mem_k, cache_mem_v, norm_mix_g, w_in, a_norm_g, a_ws, a_bs, b_dw_w, b_dw_b, b_ln_g, b_ln_b, c_qa_g, c_w_uq, c_kva_g, c_w_ukv, c_qn_g, c_qr_g, c_kn_g, c_kr_g, w_out, norm_mem_g, mem_norm_g, w_mq, w_mk, w_mv, w_mo, m_q_g, m_k_g, norm_ffn_g, w_ff1, w_ff2):
    W = dict(norm_mix_g=norm_mix_g, w_in=w_in, a_norm_g=a_norm_g, a_ws=a_ws, a_bs=a_bs,
             b_dw_w=b_dw_w, b_dw_b=b_dw_b, b_ln_g=b_ln_g, b_ln_b=b_ln_b, c_qa_g=c_qa_g,
             c_w_uq=c_w_uq, c_kva_g=c_kva_g, c_w_ukv=c_w_ukv, c_qn_g=c_qn_g, c_qr_g=c_qr_g,
             c_kn_g=c_kn_g, c_kr_g=c_kr_g, w_out=w_out, norm_mem_g=norm_mem_g,
             mem_norm_g=mem_norm_g, w_mq=w_mq, w_mk=w_mk, w_mv=w_mv, w_mo=w_mo, m_q_g=m_q_g,
             m_k_g=m_k_g, norm_ffn_g=norm_ffn_g, w_ff1=w_ff1, w_ff2=w_ff2)
    depth = w_in.shape[0]
    Bp, Tp, _ = x_prompt.shape
    Bs, Ts, _ = x_sample.shape
    past = cache_mla_latent.shape[2]
    M = mem_prompt.shape[1]
    MW = MEM_HEADS * MEM_HDIM

    tm_p = min(512, Tp)
    L_p = min(Tp, A_CHUNK)
    L_s = min(Ts, A_CHUNK)
    tabs_p = _rope_tables(0, Tp)
    tabs_s = _rope_tables(past, Ts)
    hist0 = jnp.zeros((Bp, B_KW - 1, B_WIDTH), F32)

    hp, hs = x_prompt, x_sample
    lat_p, kr_p, conv_p, mk_p, mv_p = [], [], [], [], []
    lat_s, kr_s, conv_s, gv_s = [], [], [], []
    Wb = {name: W[name].astype(BF16) for name in STACKED_WEIGHTS}
    for l in range(depth):
        Pp = _prep_layer(l, W, Wb, L_p)
        Ps = Pp if L_s == L_p else dict(Pp, a_ws=W['a_ws'][l][:, :L_s, :L_s],
                                        a_bias=jnp.repeat(W['a_bs'][l][:, :L_s].T, A_GDIM, axis=1))
        mk, mv = _memkv(mem_prompt, Pp)
        hp, _, cst, lat, kr = _layer(hp, hist0, tabs_p, None, None, mk, mv, Pp,
                                     tm=tm_p, L=L_p, tq=tm_p, start=0)
        lat_p.append(lat); kr_p.append(kr); conv_p.append(cst)
        mk_p.append(mk.reshape(Bp, M, MEM_HEADS, MEM_HDIM)); mv_p.append(mv.reshape(Bp, M, MEM_HEADS, MEM_HDIM))
        hs, gv, cst, lat, kr = _layer(hs, cache_conv[l], tabs_s, cache_mla_latent[l], cache_mla_krope[l],
                                      cache_mem_k[l].reshape(Bs, M, MW), cache_mem_v[l].reshape(Bs, M, MW), Ps,
                                      tm=Ts, L=L_s, tq=Ts, start=past)
        lat_s.append(lat); kr_s.append(kr); conv_s.append(cst); gv_s.append(gv)
    return (hp, hs,
            jnp.stack(lat_p), jnp.stack(kr_p), jnp.stack(conv_p), jnp.stack(mk_p), jnp.stack(mv_p),
            jnp.stack(lat_s), jnp.stack(kr_s), jnp.stack(conv_s), jnp.stack(gv_s))
```

```python
import functools
import math

import jax
import jax.numpy as jnp
from jax import lax
from jax.experimental import pallas as pl
from jax.experimental.pallas import tpu as pltpu

F32 = jnp.float32
BF16 = jnp.bfloat16

EPS = 1e-6
CHUNK_SHIFT = 6
A_WIDTH = 256
A_GROUPS = 4
A_GDIM = 64
A_CHUNK = 128
B_WIDTH = 256
B_KW = 31
C_HEADS = 8
C_NOPE = 64
C_ROPE = 32
C_VDIM = 64
C_QRANK = 256
C_KVRANK = 128
ROPE_THETA = 10000.0
MEM_HEADS = 4
MEM_HDIM = 128
V_ROWS = C_VDIM + 16
HEAD_PAD = 128
IN_PAD = 1536
SUBLANES = 8
HIST_PAD = 32
QK_SCALE = math.log2(math.e) / math.sqrt(C_NOPE + C_ROPE)

ATTN_HEAD_GROUP = 8
VMEM_LIMIT = 56 * 1024 * 1024


def _cparams(n_axes):
    return pltpu.CompilerParams(dimension_semantics=("arbitrary",) * n_axes,
                                vmem_limit_bytes=VMEM_LIMIT)


def _rms_lane(x, g):
    return x * lax.rsqrt(jnp.mean(x * x, axis=-1, keepdims=True) + EPS) * g


def _gelu(x):
    return 0.5 * x * (1.0 + lax.erf(x * (1.0 / math.sqrt(2.0))))


def _const_operands(items):
    arrays, specs = [], []
    for it in items:
        if isinstance(it, tuple):
            arr, layer = it
            nd = arr.ndim - 1
            spec = pl.BlockSpec((None,) + arr.shape[1:], lambda *_, layer=layer, nd=nd: (layer,) + (0,) * nd,
                                pipeline_mode=pl.Buffered(1))
        else:
            arr = it
            spec = pl.BlockSpec(arr.shape, lambda *_, nd=arr.ndim: (0,) * nd, pipeline_mode=pl.Buffered(1))
        arrays.append(arr)
        specs.append(spec)
    return arrays, specs


N_INPROJ_IN = 21


def _emit_kv(lat, krp, wuk_ref, kng_ref, wuvT_ref, k_ref, vT_ref):
    latb = lat.astype(BF16)
    kf = jnp.dot(latb, wuk_ref[...], preferred_element_type=F32)
    g = kng_ref[...]
    vT = lax.dot_general(wuvT_ref[...], latb, (((1,), (1,)), ((), ())),
                         preferred_element_type=F32)
    ones = jnp.ones((V_ROWS - C_VDIM, vT.shape[1]), vT_ref.dtype)
    for hh in range(C_HEADS):
        kh = kf[:, hh * HEAD_PAD:(hh + 1) * HEAD_PAD]
        ms = jnp.sum(kh * kh, axis=-1, keepdims=True) * (1.0 / C_NOPE)
        k_ref[0, hh, 0] = (kh * lax.rsqrt(ms + EPS) * g + krp).astype(k_ref.dtype)
        vT_ref[0, hh, 0, 0:C_VDIM, :] = vT[hh * C_VDIM:(hh + 1) * C_VDIM].astype(vT_ref.dtype)
        vT_ref[0, hh, 0, C_VDIM:V_ROWS, :] = ones


def _inproj_kernel(*refs, tm, L, prompt):
    (x_ref, hist_ref, cosq_ref, sinq_ref, cosk_ref, sink_ref,
     gmix_ref, win_ref, ag_ref, aws_ref, abias_ref,
     dww_ref, dwb_ref, lng_ref, lnb_ref,
     qag_ref, wuqT_ref, qng_ref, qrg_ref, kvag_ref, krg_ref) = refs[:N_INPROJ_IN]
    n_in = N_INPROJ_IN + (3 if prompt else 0)
    ya_ref, yb_ref, qT_ref, lat_ref, kr_ref, cs_ref = refs[n_in:n_in + 6]
    zbuf = refs[-1]
    t = pl.program_id(1)
    x = x_ref[0]
    xn = _rms_lane(x, gmix_ref[...])
    h = jnp.dot(xn.astype(BF16), win_ref[...], preferred_element_type=F32)
    a_u = h[:, 0:256]
    a_v = h[:, 256:512]
    b_a = h[:, 512:768]
    b_g = h[:, 768:1024]
    c_q = h[:, 1024:1280]
    c_kv = h[:, 1280:1408]
    c_kr = h[:, 1408:1536]

    u = _gelu(a_u)
    v = _rms_lane(_gelu(a_v), ag_ref[...])
    if not prompt:
        refs[n_in + 6][0] = v
    vb = v.astype(BF16)
    row = lax.broadcasted_iota(jnp.int32, (L, L), 0)
    col = lax.broadcasted_iota(jnp.int32, (L, L), 1)
    tril = col <= row
    wcat = jnp.concatenate(
        [jnp.where(tril, aws_ref[g], 0.0).astype(BF16) for g in range(A_GROUPS)], axis=1)
    lane = lax.broadcasted_iota(jnp.int32, (L, A_WIDTH), 1)
    gmasks = [(lane >= g * A_GDIM) & (lane < (g + 1) * A_GDIM) for g in range(A_GROUPS)]
    zero_b = jnp.zeros((L, A_WIDTH), BF16)
    bias = abias_ref[...]
    for c in range(tm // L):
        vc = vb[c * L:(c + 1) * L]
        vstack = jnp.concatenate([jnp.where(gmasks[g], vc, zero_b) for g in range(A_GROUPS)], axis=0)
        mixed = jnp.dot(wcat, vstack, preferred_element_type=F32) + bias
        ya_ref[0, c * L:(c + 1) * L, :] = (u[c * L:(c + 1) * L] * mixed).astype(ya_ref.dtype)

    z = b_a * jax.nn.sigmoid(b_g)

    @pl.when(t == 0)
    def _():
        zbuf[0:8, :] = jnp.zeros((8, B_WIDTH), F32)
        zbuf[HIST_PAD - (B_KW - 1):HIST_PAD, :] = hist_ref[0]

    @pl.when(t > 0)
    def _():
        zbuf[0:HIST_PAD, :] = zbuf[tm:tm + HIST_PAD, :]

    zbuf[HIST_PAD:HIST_PAD + tm, :] = z
    zbuf[HIST_PAD + tm:HIST_PAD + tm + SUBLANES, :] = jnp.zeros((SUBLANES, B_WIDTH), F32)
    off = HIST_PAD - (B_KW - 1)
    n_win = tm + SUBLANES
    acc = None
    for a in range(SUBLANES):
        va = None
        for blk in range(HIST_PAD // SUBLANES + 1):
            j = SUBLANES * blk + a - off
            if 0 <= j < B_KW:
                term = dww_ref[j:j + 1, :] * zbuf[SUBLANES * blk:SUBLANES * blk + n_win, :]
                va = term if va is None else va + term
        if a:
            va = pltpu.roll(va, n_win - a, 0)
        acc = va[0:tm] if acc is None else acc + va[0:tm]
    acc = acc + dwb_ref[...]
    cs_ref[0] = zbuf[tm + off:tm + HIST_PAD, :]
    mu = jnp.mean(acc, axis=-1, keepdims=True)
    xc = acc - mu
    yn = xc * lax.rsqrt(jnp.mean(xc * xc, axis=-1, keepdims=True) + EPS) * lng_ref[...] + lnb_ref[...]
    yb_ref[0] = (yn * jax.nn.sigmoid(yn)).astype(yb_ref.dtype)

    qa = _rms_lane(c_q, qag_ref[...]).astype(BF16)
    qT = lax.dot_general(wuqT_ref[...], qa, (((1,), (1,)), ((), ())),
                         preferred_element_type=F32)
    cq = cosq_ref[...]
    sq = sinq_ref[...]
    gqn = qng_ref[...] * QK_SCALE
    gqr = qrg_ref[...]
    half = C_ROPE // 2
    zpad = jnp.zeros((HEAD_PAD - C_NOPE - C_ROPE, tm), F32)
    for hh in range(C_HEADS):
        blk = qT[hh * HEAD_PAD:(hh + 1) * HEAD_PAD]
        qn = blk[0:C_NOPE]
        qr = blk[C_NOPE:C_NOPE + C_ROPE]
        qn = qn * lax.rsqrt(jnp.mean(qn * qn, axis=0, keepdims=True) + EPS) * gqn
        qr = qr * lax.rsqrt(jnp.mean(qr * qr, axis=0, keepdims=True) + EPS) * gqr
        x1 = qr[0:half]
        x2 = qr[half:C_ROPE]
        r1 = (x1 * cq - x2 * sq) * QK_SCALE
        r2 = (x1 * sq + x2 * cq) * QK_SCALE
        qT_ref[0, hh] = jnp.concatenate([qn, r1, r2, zpad], axis=0).astype(qT_ref.dtype)

    lat = _rms_lane(c_kv, kvag_ref[...])
    lat_ref[0] = lat
    krn = c_kr * lax.rsqrt(jnp.sum(c_kr * c_kr, axis=-1, keepdims=True) * (1.0 / C_ROPE) + EPS) * krg_ref[...]
    lane_k = lax.broadcasted_iota(jnp.int32, krn.shape, 1)
    partner = jnp.where(lane_k < half, pltpu.roll(krn, 128 - half, 1), pltpu.roll(krn, half, 1))
    kro = krn * cosk_ref[...] + partner * sink_ref[...]
    kr_ref[0] = kro[:, 0:C_ROPE]
    if prompt:
        wuk_ref, kng_ref, wuvT_ref = refs[N_INPROJ_IN:n_in]
        k_ref, vT_ref = refs[n_in + 6:n_in + 8]
        _emit_kv(lat, pltpu.roll(kro, C_NOPE, 1), wuk_ref, kng_ref, wuvT_ref, k_ref, vT_ref)


def _inproj(x, hist, tabs, P, *, tm, L, prompt):
    B, T, D = x.shape
    nt = T // tm
    cosq, sinq, cosk, sink = tabs
    kern = functools.partial(_inproj_kernel, tm=tm, L=L, prompt=prompt)
    tile = lambda w: pl.BlockSpec((1, tm, w), lambda b, t: (b, t, 0))
    in_specs = [
        tile(D),
        pl.BlockSpec((1, B_KW - 1, B_WIDTH), lambda b, t: (b, 0, 0)),
        pl.BlockSpec((C_ROPE // 2, tm), lambda b, t: (0, t)),
        pl.BlockSpec((C_ROPE // 2, tm), lambda b, t: (0, t)),
        pl.BlockSpec((tm, 128), lambda b, t: (t, 0)),
        pl.BlockSpec((tm, 128), lambda b, t: (t, 0)),
    ]
    consts = [P['g_mix'], P['w_in'], P['a_g'], P['a_ws'], P['a_bias'],
              P['dw_w'], P['dw_b'], P['ln_g'], P['ln_b'],
              P['qa_g'], P['w_uqT'], P['qn_g'], P['qr_g'], P['kva_g'], P['kr_g']]
    if prompt:
        consts += [P['w_uk'], P['kn_g'], P['w_uvT']]
    consts, const_specs = _const_operands(consts)
    in_specs += const_specs
    out_shape = [
        jax.ShapeDtypeStruct((B, T, A_WIDTH), BF16),
        jax.ShapeDtypeStruct((B, T, B_WIDTH), BF16),
        jax.ShapeDtypeStruct((B, C_HEADS, HEAD_PAD, T), BF16),
        jax.ShapeDtypeStruct((B, T, C_KVRANK), F32),
        jax.ShapeDtypeStruct((B, T, C_ROPE), F32),
        jax.ShapeDtypeStruct((B, B_KW - 1, B_WIDTH), F32),
    ]
    out_specs = [
        tile(A_WIDTH), tile(B_WIDTH),
        pl.BlockSpec((1, C_HEADS, HEAD_PAD, tm), lambda b, t: (b, 0, 0, t)),
        tile(C_KVRANK), tile(C_ROPE),
        pl.BlockSpec((1, B_KW - 1, B_WIDTH), lambda b, t: (b, 0, 0)),
    ]
    if prompt:
        out_shape += [jax.ShapeDtypeStruct((B, C_HEADS, nt, tm, HEAD_PAD), BF16),
                      jax.ShapeDtypeStruct((B, C_HEADS, nt, V_ROWS, tm), BF16)]
        out_specs += [pl.BlockSpec((1, C_HEADS, 1, tm, HEAD_PAD), lambda b, t: (b, 0, t, 0, 0)),
                      pl.BlockSpec((1, C_HEADS, 1, V_ROWS, tm), lambda b, t: (b, 0, t, 0, 0))]
    else:
        out_shape += [jax.ShapeDtypeStruct((B, T, A_WIDTH), F32)]
        out_specs += [tile(A_WIDTH)]
    return pl.pallas_call(
        kern, grid=(B, nt), in_specs=in_specs, out_specs=out_specs, out_shape=out_shape,
        scratch_shapes=[pltpu.VMEM((tm + HIST_PAD + SUBLANES, B_WIDTH), F32)],
        compiler_params=_cparams(2), name=f"inproj_{tm}",
    )(x, hist, cosq, sinq, cosk, sink, *consts)


def _kvgen_kernel(lat_ref, krp_ref, wuk_ref, kng_ref, wuvT_ref, k_ref, vT_ref):
    _emit_kv(lat_ref[0], krp_ref[0], wuk_ref, kng_ref, wuvT_ref, k_ref, vT_ref)


def _kvgen(lat, krp, P, *, tk):
    B, Tk, _ = lat.shape
    nkt = Tk // tk
    consts, const_specs = _const_operands([P['w_uk'], P['kn_g'], P['w_uvT']])
    return pl.pallas_call(
        _kvgen_kernel, grid=(B, nkt),
        in_specs=[pl.BlockSpec((1, tk, C_KVRANK), lambda b, t: (b, t, 0)),
                  pl.BlockSpec((1, tk, HEAD_PAD), lambda b, t: (b, t, 0))]
                 + const_specs,
        out_specs=(pl.BlockSpec((1, C_HEADS, 1, tk, HEAD_PAD), lambda b, t: (b, 0, t, 0, 0)),
                   pl.BlockSpec((1, C_HEADS, 1, V_ROWS, tk), lambda b, t: (b, 0, t, 0, 0))),
        out_shape=(jax.ShapeDtypeStruct((B, C_HEADS, nkt, tk, HEAD_PAD), BF16),
                   jax.ShapeDtypeStruct((B, C_HEADS, nkt, V_ROWS, tk), BF16)),
        compiler_params=_cparams(2), name=f"kvgen_{tk}",
    )(lat, krp, *consts)


def _attn_kernel(qT_ref, k_ref, vT_ref, o_ref, *bufs, tq, tk, q_pos0, hg):
    qi = pl.program_id(2)

    def scores(hh, kt, vis):
        s = jnp.dot(k_ref[0, hh, kt], qT_ref[0, hh], preferred_element_type=F32)
        if vis is not None:
            s = jnp.where(vis, s, -1e30)
        bufs[hh][...] = s
        return jnp.max(s, axis=0, keepdims=True)

    def update(hh, kt, cmax, state):
        m, acc = state
        m_new = jnp.maximum(m, cmax)
        alpha = jnp.exp2(m - m_new)
        p = jnp.exp2(bufs[hh][...] - m_new)
        pv = jnp.dot(vT_ref[0, hh, kt], p.astype(BF16), preferred_element_type=F32)
        return m_new, alpha * acc + pv

    state0 = (jnp.full((1, tq), -1e30, F32), jnp.zeros((V_ROWS, tq), F32))
    last = hg - 1

    def sweep(kt, vis, c_pend, kt_pend, states):
        states = list(states)
        for hh in range(hg):
            c_new = scores(hh, kt, vis)
            if hh == 0:
                if c_pend is not None:
                    states[last] = update(last, kt_pend, c_pend, states[last])
            else:
                states[hh - 1] = update(hh - 1, kt, c_prev, states[hh - 1])
            c_prev = c_new
        return c_prev, kt, tuple(states)

    kpos = qi * tk + lax.broadcasted_iota(jnp.int32, (tk, 1), 0)
    qpos = q_pos0 + qi * tq + lax.broadcasted_iota(jnp.int32, (1, tq), 1)
    vis_diag = lax.shift_right_logical(kpos, CHUNK_SHIFT) <= lax.shift_right_logical(qpos, CHUNK_SHIFT)
    carry = sweep(qi, vis_diag, None, None, (state0,) * hg)

    def quad(i, carry):
        for u in range(4):
            carry = sweep(4 * i + u, None, *carry)
        return carry

    n_quads = lax.shift_right_logical(qi, 2)
    carry = lax.fori_loop(0, n_quads, quad, carry)
    base = n_quads * 4
    carry = lax.cond((qi & 2) != 0, lambda c: sweep(base + 1, None, *sweep(base, None, *c)), lambda c: c, carry)
    carry = lax.cond((qi & 1) != 0, lambda c: sweep(base + (qi & 2), None, *c), lambda c: c, carry)
    c_pend, kt_pend, states = carry
    states = states[:last] + (update(last, kt_pend, c_pend, states[last]),)
    for hh, (_, acc) in enumerate(states):
        inv_l = 1.0 / acc[C_VDIM:C_VDIM + 1]
        o_ref[0, hh * C_VDIM:(hh + 1) * C_VDIM, :] = (acc[0:C_VDIM] * inv_l).astype(o_ref.dtype)


def _attn(qT, k, vT, *, tq, q_pos0):
    B, H, _, Tq = qT.shape
    _, _, nkt, tk, _ = k.shape
    nq = Tq // tq
    hg = ATTN_HEAD_GROUP
    kern = functools.partial(_attn_kernel, tq=tq, tk=tk, q_pos0=q_pos0, hg=hg)
    return pl.pallas_call(
        kern, grid=(B, H // hg, nq),
        in_specs=[pl.BlockSpec((1, hg, HEAD_PAD, tq), lambda b, h, q: (b, h, 0, q)),
                  pl.BlockSpec((1, hg, nkt, tk, HEAD_PAD), lambda b, h, q: (b, h, 0, 0, 0),
                               pipeline_mode=pl.Buffered(1)),
                  pl.BlockSpec((1, hg, nkt, V_ROWS, tk), lambda b, h, q: (b, h, 0, 0, 0),
                               pipeline_mode=pl.Buffered(1))],
        out_specs=pl.BlockSpec((1, hg * C_VDIM, tq), lambda b, h, q: (b, h, q)),
        out_shape=jax.ShapeDtypeStruct((B, H * C_VDIM, Tq), BF16),
        scratch_shapes=[pltpu.VMEM((tk, tq), F32) for _ in range(hg)],
        compiler_params=_cparams(3), name=f"attn_{tq}",
    )(qT, k, vT)


def _post_kernel(x_ref, ya_ref, yb_ref, ycT_ref, mk_ref, mv_ref,
                 wout_ref, gmem_ref, wmq_ref, mqg_ref, wmo_ref, gffn_ref, w1_ref, w2_ref,
                 o_ref, *, ff_chunk):
    nb, tm, D = x_ref.shape
    ab = A_WIDTH + B_WIDTH
    x1 = []
    for b in range(nb):
        mix_ab = jnp.concatenate([ya_ref[b], yb_ref[b]], axis=1)
        xb = x_ref[b] + jnp.dot(mix_ab, wout_ref[0:ab, :], preferred_element_type=F32)
        x1.append(xb + lax.dot_general(ycT_ref[b], wout_ref[ab:, :], (((0,), (0,)), ((), ())),
                                       preferred_element_type=F32))
    x1 = x1[0] if nb == 1 else jnp.concatenate(x1, axis=0)
    qm = jnp.dot(_rms_lane(x1, gmem_ref[...]).astype(BF16), wmq_ref[...], preferred_element_type=F32)
    om = []
    for b in range(nb):
        mk = mk_ref[b].astype(BF16)
        mv = mv_ref[b].astype(BF16)
        outs = []
        for hh in range(MEM_HEADS):
            sl = slice(hh * MEM_HDIM, (hh + 1) * MEM_HDIM)
            qh = _rms_lane(qm[b * tm:(b + 1) * tm, sl], mqg_ref[...]).astype(BF16)
            s = lax.dot_general(qh, mk[:, sl], (((1,), (1,)), ((), ())),
                                preferred_element_type=F32) * (1.0 / math.sqrt(MEM_HDIM))
            s = s - jnp.max(s, axis=-1, keepdims=True)
            p = jnp.exp(s)
            p = p * (1.0 / jnp.sum(p, axis=-1, keepdims=True))
            outs.append(jnp.dot(p.astype(BF16), mv[:, sl], preferred_element_type=F32))
        om.append(jnp.concatenate(outs, axis=1).astype(BF16))
    om = om[0] if nb == 1 else jnp.concatenate(om, axis=0)
    x2 = x1 + jnp.dot(om, wmo_ref[...], preferred_element_type=F32)
    xn = _rms_lane(x2, gffn_ref[...]).astype(BF16)
    acc = x2
    d_ff = w1_ref.shape[1]
    for c in range(d_ff // ff_chunk):
        hc = jnp.dot(xn, w1_ref[:, c * ff_chunk:(c + 1) * ff_chunk], preferred_element_type=F32)
        hc = jnp.square(jnp.maximum(hc, 0.0)).astype(BF16)
        acc = acc + jnp.dot(hc, w2_ref[c * ff_chunk:(c + 1) * ff_chunk, :], preferred_element_type=F32)
    for b in range(nb):
        o_ref[b] = acc[b * tm:(b + 1) * tm]


def _post(x, ya, yb, ycT, mk, mv, P, *, tm):
    B, T, D = x.shape
    nt = T // tm
    M = mk.shape[1]
    nb = B if nt == 1 else 1
    consts, const_specs = _const_operands(
        [P['w_out'], P['g_mem'], P['w_mq'], P['mq_g'], P['w_mo'], P['g_ffn'], P['w_ff1'], P['w_ff2']])
    tile = lambda w: pl.BlockSpec((nb, tm, w), lambda b, t: (b, t, 0))
    return pl.pallas_call(
        functools.partial(_post_kernel, ff_chunk=1024), grid=(B // nb, nt),
        in_specs=[tile(D), tile(A_WIDTH), tile(B_WIDTH),
                  pl.BlockSpec((nb, C_HEADS * C_VDIM, tm), lambda b, t: (b, 0, t)),
                  pl.BlockSpec((nb, M, MEM_HEADS * MEM_HDIM), lambda b, t: (b, 0, 0)),
                  pl.BlockSpec((nb, M, MEM_HEADS * MEM_HDIM), lambda b, t: (b, 0, 0))]
                 + const_specs,
        out_specs=tile(D),
        out_shape=jax.ShapeDtypeStruct((B, T, D), F32),
        compiler_params=_cparams(2), name=f"post_{tm}",
    )(x, ya, yb, ycT, mk, mv, *consts)


def _memkv_kernel(mem_ref, g_ref, wk_ref, wv_ref, kg_ref, k_ref, v_ref):
    mn = _rms_lane(mem_ref[0], g_ref[...]).astype(BF16)
    k = jnp.dot(mn, wk_ref[...], preferred_element_type=F32)
    v_ref[0] = jnp.dot(mn, wv_ref[...], preferred_element_type=F32)
    for hh in range(MEM_HEADS):
        sl = slice(hh * MEM_HDIM, (hh + 1) * MEM_HDIM)
        k_ref[0, :, sl] = _rms_lane(k[:, sl], kg_ref[...])


def _memkv(mem, P):
    B, M, D = mem.shape
    W = MEM_HEADS * MEM_HDIM
    consts, const_specs = _const_operands([P['g_memkv'], P['w_mk'], P['w_mv'], P['mk_g']])
    return pl.pallas_call(
        _memkv_kernel, grid=(B,),
        in_specs=[pl.BlockSpec((1, M, D), lambda b: (b, 0, 0))] + const_specs,
        out_specs=(pl.BlockSpec((1, M, W), lambda b: (b, 0, 0)), pl.BlockSpec((1, M, W), lambda b: (b, 0, 0))),
        out_shape=(jax.ShapeDtypeStruct((B, M, W), F32), jax.ShapeDtypeStruct((B, M, W), F32)),
        compiler_params=_cparams(1), name="memkv",
    )(mem, *consts)


def _row(v):
    return v.reshape(1, -1).astype(F32)


STACKED_WEIGHTS = ('w_out', 'w_mq', 'w_mo', 'w_ff1', 'w_ff2', 'w_mk', 'w_mv')


def _prep_layer(l, W, Wb, L):
    P = {}
    P['g_mix'] = _row(W['norm_mix_g'][l])
    P['w_in'] = jnp.pad(W['w_in'][l], ((0, 0), (0, IN_PAD - W['w_in'].shape[2]))).astype(BF16)
    P['a_g'] = _row(W['a_norm_g'][l])
    P['a_ws'] = W['a_ws'][l][:, :L, :L]
    P['a_bias'] = jnp.repeat(W['a_bs'][l][:, :L].T, A_GDIM, axis=1)
    P['dw_w'] = W['b_dw_w'][l]
    P['dw_b'] = _row(W['b_dw_b'][l])
    P['ln_g'] = _row(W['b_ln_g'][l])
    P['ln_b'] = _row(W['b_ln_b'][l])
    P['qa_g'] = _row(W['c_qa_g'][l])
    wq = W['c_w_uq'][l].reshape(C_QRANK, C_HEADS, C_NOPE + C_ROPE)
    wq = jnp.pad(wq, ((0, 0), (0, 0), (0, HEAD_PAD - C_NOPE - C_ROPE)))
    P['w_uqT'] = wq.reshape(C_QRANK, C_HEADS * HEAD_PAD).T.astype(BF16)
    P['qn_g'] = W['c_qn_g'][l].reshape(C_NOPE, 1)
    P['qr_g'] = W['c_qr_g'][l].reshape(C_ROPE, 1)
    P['kva_g'] = _row(W['c_kva_g'][l])
    P['kr_g'] = jnp.pad(_row(W['c_kr_g'][l]), ((0, 0), (0, 128 - C_ROPE)))
    wkv = W['c_w_ukv'][l].reshape(C_KVRANK, C_HEADS, C_NOPE + C_VDIM)
    P['w_uk'] = jnp.pad(wkv[:, :, :C_NOPE], ((0, 0), (0, 0), (0, HEAD_PAD - C_NOPE))
                        ).reshape(C_KVRANK, C_HEADS * HEAD_PAD).astype(BF16)
    P['w_uvT'] = wkv[:, :, C_NOPE:].reshape(C_KVRANK, C_HEADS * C_VDIM).T.astype(BF16)
    P['kn_g'] = jnp.pad(_row(W['c_kn_g'][l]), ((0, 0), (0, HEAD_PAD - C_NOPE)))
    P['g_mem'] = _row(W['norm_mem_g'][l])
    P['mq_g'] = _row(W['m_q_g'][l])
    P['g_ffn'] = _row(W['norm_ffn_g'][l])
    P['g_memkv'] = _row(W['mem_norm_g'][l])
    P['mk_g'] = _row(W['m_k_g'][l])
    for name in STACKED_WEIGHTS:
        P[name] = (Wb[name], l)
    return P


def _rope_tables(start, t):
    half = C_ROPE // 2
    inv = ROPE_THETA ** (-jnp.arange(half, dtype=F32) / half)
    ang = (start + jnp.arange(t)).astype(F32)[:, None] * inv[None, :]
    cos, sin = jnp.cos(ang), jnp.sin(ang)
    zeros = jnp.zeros((t, 128 - C_ROPE), F32)
    cosk = jnp.concatenate([cos, cos, zeros], axis=1)
    sink = jnp.concatenate([-sin, sin, zeros], axis=1)
    return cos.T, sin.T, cosk, sink


def _pad_kr(kr):
    return jnp.pad(kr, ((0, 0), (0, 0), (C_NOPE, HEAD_PAD - C_NOPE - C_ROPE)))


def _layer(x, hist, tabs, lat_past, kr_past, mk, mv, P, *, tm, L, tq, start):
    if lat_past is None:
        ya, yb, qT, lat, kr, cstate, k, vT = _inproj(x, hist, tabs, P, tm=tm, L=L, prompt=True)
        gv = None
    else:
        ya, yb, qT, lat, kr, cstate, gv = _inproj(x, hist, tabs, P, tm=tm, L=L, prompt=False)
        lat_all = jnp.concatenate([lat_past, lat], axis=1)
        kr_all = jnp.concatenate([kr_past, kr], axis=1)
        k, vT = _kvgen(lat_all, _pad_kr(kr_all), P, tk=lat_all.shape[1])
    ycT = _attn(qT, k, vT, tq=tq, q_pos0=start)
    y = _post(x, ya, yb, ycT, mk, mv, P, tm=tm)
    return y, gv, cstate, lat, kr


def kernel(x_prompt, x_sample, mem_prompt, cache_mla_latent, cache_mla_krope, cache_conv, cache_mem_k, cache_mem_v, norm_mix_g, w_in, a_norm_g, a_ws, a_bs, b_dw_w, b_dw_b, b_ln_g, b_ln_b, c_qa_g, c_w_uq, c_kva_g, c_w_ukv, c_qn_g, c_qr_g, c_kn_g, c_kr_g, w_out, norm_mem_g, mem_norm_g, w_mq, w_mk, w_mv, w_mo, m_q_g, m_k_g, norm_ffn_g, w_ff1, w_ff2):
    W = dict(norm_mix_g=norm_mix_g, w_in=w_in, a_norm_g=a_norm_g, a_ws=a_ws, a_bs=a_bs,
             b_dw_w=b_dw_w, b_dw_b=b_dw_b, b_ln_g=b_ln_g, b_ln_b=b_ln_b, c_qa_g=c_qa_g,
             c_w_uq=c_w_uq, c_kva_g=c_kva_g, c_w_ukv=c_w_ukv, c_qn_g=c_qn_g, c_qr_g=c_qr_g,
             c_kn_g=c_kn_g, c_kr_g=c_kr_g, w_out=w_out, norm_mem_g=norm_mem_g,
             mem_norm_g=mem_norm_g, w_mq=w_mq, w_mk=w_mk, w_mv=w_mv, w_mo=w_mo, m_q_g=m_q_g,
             m_k_g=m_k_g, norm_ffn_g=norm_ffn_g, w_ff1=w_ff1, w_ff2=w_ff2)
    depth = w_in.shape[0]
    Bp, Tp, _ = x_prompt.shape
    Bs, Ts, _ = x_sample.shape
    past = cache_mla_latent.shape[2]
    M = mem_prompt.shape[1]
    MW = MEM_HEADS * MEM_HDIM

    tm_p = min(512, Tp)
    L_p = min(Tp, A_CHUNK)
    L_s = min(Ts, A_CHUNK)
    tabs_p = _rope_tables(0, Tp)
    tabs_s = _rope_tables(past, Ts)
    hist0 = jnp.zeros((Bp, B_KW - 1, B_WIDTH), F32)

    hp, hs = x_prompt, x_sample
    lat_p, kr_p, conv_p, mk_p, mv_p = [], [], [], [], []
    lat_s, kr_s, conv_s, gv_s = [], [], [], []
    Wb = {name: W[name].astype(BF16) for name in STACKED_WEIGHTS}
    for l in range(depth):
        Pp = _prep_layer(l, W, Wb, L_p)
        Ps = Pp if L_s == L_p else dict(Pp, a_ws=W['a_ws'][l][:, :L_s, :L_s],
                                        a_bias=jnp.repeat(W['a_bs'][l][:, :L_s].T, A_GDIM, axis=1))
        mk, mv = _memkv(mem_prompt, Pp)
        hp, _, cst, lat, kr = _layer(hp, hist0, tabs_p, None, None, mk, mv, Pp,
                                     tm=tm_p, L=L_p, tq=tm_p, start=0)
        lat_p.append(lat); kr_p.append(kr); conv_p.append(cst)
        mk_p.append(mk.reshape(Bp, M, MEM_HEADS, MEM_HDIM)); mv_p.append(mv.reshape(Bp, M, MEM_HEADS, MEM_HDIM))
        hs, gv, cst, lat, kr = _layer(hs, cache_conv[l], tabs_s, cache_mla_latent[l], cache_mla_krope[l],
                                      cache_mem_k[l].reshape(Bs, M, MW), cache_mem_v[l].reshape(Bs, M, MW), Ps,
                                      tm=Ts, L=L_s, tq=Ts, start=past)
        lat_s.append(lat); kr_s.append(kr); conv_s.append(cst); gv_s.append(gv)
    return (hp, hs,
            jnp.stack(lat_p), jnp.stack(kr_p), jnp.stack(conv_p), jnp.stack(mk_p), jnp.stack(mv_p),
            jnp.stack(lat_s), jnp.stack(kr_s), jnp.stack(conv_s), jnp.stack(gv_s))
```
